```python
import jax, jax.numpy as jnp
from jax import lax
import numpy as np

D_MODEL = 2048
BATCH = 2
SEQ = 8192
DEPTH = 4
DEC_BATCH = 8
DEC_SEQ = 4096
PAST_LEN = 128

GRID_W = 64
N_MEM = 256
LRU_WIDTH = D_MODEL // 2
LRU_BLOCKS = 16
LRU_BLOCK = LRU_WIDTH // LRU_BLOCKS
LRU_C = 8.0
CONV_W = 4
CONV_PAD = (2, 1)
NA_WIDTH = D_MODEL // 4
NA_HEAD_DIM = 64
NA_HEADS = NA_WIDTH // NA_HEAD_DIM
NA_WIN_ROWS = 8
NA_WIN_COLS = 16
NA_COL_BLOCK = 16
NA_KEY_BAND = 32
NA_N_COL_BLOCKS = GRID_W // NA_COL_BLOCK
MASK_VALUE = -1e30
MEM_WIDTH = D_MODEL // 4
MEM_HEADS = 4
MEM_HEAD_DIM = MEM_WIDTH // MEM_HEADS
MIX_WIDTH = LRU_WIDTH + NA_WIDTH + MEM_WIDTH
IN_WIDTH = 2 * LRU_WIDTH + 3 * NA_WIDTH + MEM_WIDTH
IN_SPLITS = (LRU_WIDTH, 2 * LRU_WIDTH, 2 * LRU_WIDTH + NA_WIDTH,
             2 * LRU_WIDTH + 2 * NA_WIDTH, 2 * LRU_WIDTH + 3 * NA_WIDTH)
D_FF = 4096
N_EXPERTS = 8
TOP_K = 2
D_FF_EXPERT = 1024
N_DENSE = (DEPTH + 1) // 2
N_MOE = DEPTH // 2
DN_ALPHA = (2 * DEPTH) ** 0.25
DN_BETA = (8 * DEPTH) ** -0.25
LN_EPS = 1e-5
RMS_EPS = 1e-6

kernel_name = 'hymba_rglru_natten_memory_encoder'


def layer_norm(x, g, b):
    xf = x.astype(jnp.float32)
    mu = jnp.mean(xf, -1, keepdims=True)
    var = jnp.mean(jnp.square(xf - mu), -1, keepdims=True)
    return ((xf - mu) * lax.rsqrt(var + LN_EPS) * g.astype(jnp.float32)
            + b.astype(jnp.float32)).astype(x.dtype)


def rms_normalize(y):
    yf = y.astype(jnp.float32)
    return yf * lax.rsqrt(jnp.mean(jnp.square(yf), -1, keepdims=True) + RMS_EPS)


def depthwise_conv(x, w, b):
    y = lax.conv_general_dilated(
        x, w[:, None, :].astype(x.dtype), window_strides=(1,), padding=[CONV_PAD],
        dimension_numbers=('NWC', 'WIO', 'NWC'), feature_group_count=x.shape[-1])
    return y + b.astype(x.dtype)


def rglru_direction(xc, wg, bg, lam, reverse):
    B, T, _ = xc.shape
    xb = xc.reshape(B, T, LRU_BLOCKS, LRU_BLOCK)
    gates = jnp.einsum('btnh,gnhk->gbtnk', xb, wg.astype(jnp.float32)) \
        + bg.astype(jnp.float32)[:, None, None]
    r = jax.nn.sigmoid(gates[0]).reshape(B, T, LRU_WIDTH)
    i = jax.nn.sigmoid(gates[1]).reshape(B, T, LRU_WIDTH)
    log_a = -LRU_C * r * jax.nn.softplus(-lam.astype(jnp.float32))
    a = jnp.exp(log_a)
    u = jnp.sqrt(-jnp.expm1(2.0 * log_a)) * (i * xc)

    def step(h, inp):
        a_t, u_t = inp
        h = a_t * h + u_t
        return h, h

    h0 = jnp.zeros((B, LRU_WIDTH), jnp.float32)
    _, hs = lax.scan(step, h0, (jnp.swapaxes(a, 0, 1), jnp.swapaxes(u, 0, 1)), reverse=reverse)
    return jnp.swapaxes(hs, 0, 1)


def _col_tables():
    wc = NA_WIN_COLS
    c = np.arange(GRID_W)
    start = np.clip(c - wc // 2, 0, GRID_W - wc)
    qb = np.arange(NA_N_COL_BLOCKS)
    band_start = np.clip(qb * NA_COL_BLOCK - wc // 2, 0, GRID_W - NA_KEY_BAND)
    kcol = band_start[:, None] + np.arange(NA_KEY_BAND)[None, :]
    qcol = c.reshape(NA_N_COL_BLOCKS, NA_COL_BLOCK)
    st = start.reshape(NA_N_COL_BLOCKS, NA_COL_BLOCK)[:, :, None]
    kc = kcol[:, None, :]
    valid = (kc >= st) & (kc < st + wc)
    dc = np.clip(kc - qcol[:, :, None] + (NA_WIN_COLS - 1), 0, 2 * NA_WIN_COLS - 2)
    return kcol, valid, dc


def neighbourhood_attention(q, k, v, rpb):
    B, T, _ = q.shape
    rows = T // GRID_W
    wr = min(NA_WIN_ROWS, rows)
    shp = (B, rows, GRID_W, NA_HEADS, NA_HEAD_DIM)
    qg = (q * NA_HEAD_DIM ** -0.5).reshape(shp)
    kg = k.reshape(shp)
    vg = v.reshape(shp)
    kcol_np, valid_np, dc_np = _col_tables()
    kcol = jnp.asarray(kcol_np)
    valid = jnp.asarray(valid_np)[:, :, None, :]
    dc = jnp.asarray(dc_np)[:, :, None, :]
    rpb32 = rpb.astype(jnp.float32)

    def one_row(r):
        rs = jnp.clip(r - wr // 2, 0, rows - wr)
        k_band = lax.dynamic_slice_in_dim(kg, rs, wr, axis=1)[:, :, kcol]
        v_band = lax.dynamic_slice_in_dim(vg, rs, wr, axis=1)[:, :, kcol]
        q_row = lax.dynamic_index_in_dim(qg, r, axis=1, keepdims=False)
        q_row = q_row.reshape(B, NA_N_COL_BLOCKS, NA_COL_BLOCK, NA_HEADS, NA_HEAD_DIM)
        dr = (rs + jnp.arange(wr) - r + (NA_WIN_ROWS - 1))[None, None, :, None]
        bias = rpb32[:, dr, dc]
        s = jnp.einsum('bnqhd,bwnkhd->bhnqwk', q_row, k_band).astype(jnp.float32) + bias[None]
        s = jnp.where(valid, s, MASK_VALUE)
        p = jax.nn.softmax(s.reshape(s.shape[:4] + (wr * NA_KEY_BAND,)), axis=-1)
        p = p.reshape(s.shape).astype(v.dtype)
        o = jnp.einsum('bhnqwk,bwnkhd->bnqhd', p, v_band)
        return o.reshape(B, GRID_W, NA_HEADS, NA_HEAD_DIM)

    out = lax.map(one_row, jnp.arange(rows))
    return jnp.moveaxis(out, 0, 1).reshape(B, T, NA_WIDTH)


def memory_attention(qm, mem, w_kv):
    B, T, _ = qm.shape
    kv = mem @ w_kv
    k, v = jnp.split(kv, 2, axis=-1)
    q = (qm * MEM_HEAD_DIM ** -0.5).reshape(B, T, MEM_HEADS, MEM_HEAD_DIM)
    k = k.reshape(B, -1, MEM_HEADS, MEM_HEAD_DIM)
    v = v.reshape(B, -1, MEM_HEADS, MEM_HEAD_DIM)
    s = jnp.einsum('bthd,bmhd->bhtm', q, k).astype(jnp.float32)
    p = jax.nn.softmax(s, axis=-1).astype(v.dtype)
    return jnp.einsum('bhtm,bmhd->bthd', p, v).reshape(B, T, MEM_WIDTH)


def mixer(x, mem, w_in, conv_w, conv_b, lru_gate_w, lru_gate_b, lru_lambda, rpb,
          w_mem_kv, mix_gain, w_out):
    u = x @ w_in
    xl, gl, q, k, v, qm = jnp.split(u, IN_SPLITS, axis=-1)
    xc = depthwise_conv(xl, conv_w, conv_b).astype(jnp.float32)
    h = (rglru_direction(xc, lru_gate_w[0], lru_gate_b[0], lru_lambda[0], False)
         + rglru_direction(xc, lru_gate_w[1], lru_gate_b[1], lru_lambda[1], True))
    y_lru = h.astype(x.dtype) * jax.nn.gelu(gl)
    y_na = neighbourhood_attention(q, k, v, rpb)
    y_mem = memory_attention(qm, mem, w_mem_kv)
    y = jnp.concatenate([rms_normalize(y_lru), rms_normalize(y_na), rms_normalize(y_mem)], -1)
    y = (y * mix_gain.astype(jnp.float32)).astype(x.dtype)
    return y @ w_out


def swiglu(x, w1, w3, w2):
    return (jax.nn.silu(x @ w1) * (x @ w3)) @ w2


def moe_swiglu(x, router_w, w1, w3, w2):
    B, T, D = x.shape
    xt = x.reshape(B * T, D)
    logits = (xt @ router_w).astype(jnp.float32)
    top_v, top_i = lax.top_k(logits, TOP_K)
    top_p = jax.nn.softmax(top_v, axis=-1)
    gates = jnp.sum(jax.nn.one_hot(top_i, N_EXPERTS, dtype=jnp.float32) * top_p[..., None], axis=1)
    gates = gates.astype(x.dtype)
    y = jnp.zeros_like(xt)
    for e in range(N_EXPERTS):
        y = y + gates[:, e:e + 1] * swiglu(xt, w1[e], w3[e], w2[e])
    return y.reshape(B, T, D)


def trunk(x, mem, w_in, conv_w, conv_b, lru_gate_w, lru_gate_b, lru_lambda, rpb, w_mem_kv,
          mix_gain, w_out, ln1_g, ln1_b, ffn_w1, ffn_w3, ffn_w2, router_w, moe_w1, moe_w3,
          moe_w2, ln2_g, ln2_b):
    for l in range(DEPTH):
        o = mixer(x, mem, w_in[l], conv_w[l], conv_b[l], lru_gate_w[l], lru_gate_b[l],
                  lru_lambda[l], rpb[l], w_mem_kv[l], mix_gain[l], w_out[l])
        x = layer_norm(DN_ALPHA * x + o, ln1_g[l], ln1_b[l])
        j = l // 2
        if l % 2 == 0:
            f = swiglu(x, ffn_w1[j], ffn_w3[j], ffn_w2[j])
        else:
            f = moe_swiglu(x, router_w[j], moe_w1[j], moe_w3[j], moe_w2[j])
        x = layer_norm(DN_ALPHA * x + f, ln2_g[l], ln2_b[l])
    return x


def setup_inputs(seed: int = 0) -> dict:
    key = jax.random.key(seed)
    ks = jax.random.split(key, 32)
    f32 = jnp.float32

    def nrm(k, shape, scale):
        return jax.random.normal(k, shape, f32) * scale

    u = jax.random.uniform(ks[9], (DEPTH, 2, LRU_WIDTH), f32, minval=0.9, maxval=0.999)
    a = u ** (1.0 / LRU_C)
    lru_lambda = jnp.log(a) - jnp.log1p(-a)
    return {
        'x_prompt': nrm(ks[0], (BATCH, SEQ, D_MODEL), 1.0),
        'x_sample': nrm(ks[1], (DEC_BATCH, DEC_SEQ, D_MODEL), 1.0),
        'mem_prompt': nrm(ks[2], (BATCH, N_MEM, D_MODEL), 1.0),
        'mem_sample': nrm(ks[3], (DEC_BATCH, N_MEM, D_MODEL), 1.0),
        'w_in': nrm(ks[4], (DEPTH, D_MODEL, IN_WIDTH), D_MODEL ** -0.5),
        'conv_w': nrm(ks[5], (DEPTH, CONV_W, LRU_WIDTH), CONV_W ** -0.5),
        'conv_b': nrm(ks[6], (DEPTH, LRU_WIDTH), 0.02),
        'lru_gate_w': nrm(ks[7], (DEPTH, 2, 2, LRU_BLOCKS, LRU_BLOCK, LRU_BLOCK), LRU_BLOCK ** -0.5),
        'lru_gate_b': nrm(ks[8], (DEPTH, 2, 2, LRU_BLOCKS, LRU_BLOCK), 0.1),
        'lru_lambda': lru_lambda,
        'rpb': nrm(ks[10], (DEPTH, NA_HEADS, 2 * NA_WIN_ROWS - 1, 2 * NA_WIN_COLS - 1), 0.1),
        'w_mem_kv': nrm(ks[11], (DEPTH, D_MODEL, 2 * MEM_WIDTH), D_MODEL ** -0.5),
        'mix_gain': 1.0 + nrm(ks[12], (DEPTH, MIX_WIDTH), 0.02),
        'w_out': nrm(ks[13], (DEPTH, MIX_WIDTH, D_MODEL), DN_BETA * MIX_WIDTH ** -0.5),
        'ln1_g': 1.0 + nrm(ks[14], (DEPTH, D_MODEL), 0.02),
        'ln1_b': nrm(ks[15], (DEPTH, D_MODEL), 0.02),
        'ffn_w1': nrm(ks[16], (N_DENSE, D_MODEL, D_FF), D_MODEL ** -0.5),
        'ffn_w3': nrm(ks[17], (N_DENSE, D_MODEL, D_FF), D_MODEL ** -0.5),
        'ffn_w2': nrm(ks[18], (N_DENSE, D_FF, D_MODEL), DN_BETA * D_FF ** -0.5),
        'router_w': nrm(ks[19], (N_MOE, D_MODEL, N_EXPERTS), D_MODEL ** -0.5),
        'moe_w1': nrm(ks[20], (N_MOE, N_EXPERTS, D_MODEL, D_FF_EXPERT), D_MODEL ** -0.5),
        'moe_w3': nrm(ks[21], (N_MOE, N_EXPERTS, D_MODEL, D_FF_EXPERT), D_MODEL ** -0.5),
        'moe_w2': nrm(ks[22], (N_MOE, N_EXPERTS, D_FF_EXPERT, D_MODEL), DN_BETA * D_FF_EXPERT ** -0.5),
        'ln2_g': 1.0 + nrm(ks[23], (DEPTH, D_MODEL), 0.02),
        'ln2_b': nrm(ks[24], (DEPTH, D_MODEL), 0.02),
    }


def reference(x_prompt, x_sample, mem_prompt, mem_sample, w_in, conv_w, conv_b, lru_gate_w,
              lru_gate_b, lru_lambda, rpb, w_mem_kv, mix_gain, w_out, ln1_g, ln1_b, ffn_w1,
              ffn_w3, ffn_w2, router_w, moe_w1, moe_w3, moe_w2, ln2_g, ln2_b):
    params = (w_in, conv_w, conv_b, lru_gate_w, lru_gate_b, lru_lambda, rpb, w_mem_kv,
              mix_gain, w_out, ln1_g, ln1_b, ffn_w1, ffn_w3, ffn_w2, router_w, moe_w1,
              moe_w3, moe_w2, ln2_g, ln2_b)
    y_prompt = trunk(x_prompt, mem_prompt, *params)
    y_sample = trunk(x_sample, mem_sample, *params)
    return (y_prompt, y_sample)
```

```python
import functools

import numpy as np
import jax
import jax.numpy as jnp
from jax import lax
from jax.experimental import pallas as pl
from jax.experimental.pallas import tpu as pltpu

F32 = jnp.float32
BF16 = jnp.bfloat16

D_MODEL = 2048
DEPTH = 4
GRID_W = 64
LRU_WIDTH = 1024
LRU_BLOCKS = 16
LRU_BLOCK = 64
LRU_C = 8.0
NA_WIDTH = 512
NA_HEADS = 8
NA_HEAD_DIM = 64
NA_WIN_ROWS = 8
NA_WIN_COLS = 16
MASK_VALUE = -1e30
MEM_WIDTH = 512
MEM_HEADS = 4
MEM_HEAD_DIM = 128
N_MEM = 256
IN_WIDTH = 4096
D_FF = 4096
N_EXPERTS = 8
D_FF_EXPERT = 1024
DN_ALPHA = (2 * DEPTH) ** 0.25
LN_EPS = 1e-5
RMS_EPS = 1e-6

VMEM_LIMIT_BYTES = 56 * 1024 * 1024
GATE_GROUP = 256
N_GATE_GROUPS = LRU_WIDTH // GATE_GROUP
SUBLANES = 8


def _params(*semantics):
    return pltpu.CompilerParams(dimension_semantics=semantics,
                                vmem_limit_bytes=VMEM_LIMIT_BYTES)


def _layer_norm_rows(z, g, b):
    mu = jnp.mean(z, axis=-1, keepdims=True)
    zc = z - mu
    var = jnp.mean(zc * zc, axis=-1, keepdims=True)
    return zc * lax.rsqrt(var + LN_EPS) * g + b


def _rms_gain_rows(y, gain):
    ms = jnp.mean(y * y, axis=-1, keepdims=True)
    return y * lax.rsqrt(ms + RMS_EPS) * gain


def _in_proj_kernel(x_ref, w_ref, lru_ref, att_ref, xb_ref):
    j = pl.program_id(1)

    @pl.when(j == 0)
    def _():
        xb_ref[...] = x_ref[...].astype(BF16)

    y = jnp.dot(xb_ref[...], w_ref[...], preferred_element_type=F32)

    @pl.when(j < 2)
    def _():
        lru_ref[...] = y

    @pl.when(j >= 2)
    def _():
        att_ref[...] = y.astype(BF16)


def _in_proj(x, w):
    n = x.shape[0]
    tm = min(1024, n)
    tn = 1024
    return pl.pallas_call(
        _in_proj_kernel,
        grid=(n // tm, IN_WIDTH // tn),
        in_specs=[pl.BlockSpec((tm, D_MODEL), lambda i, j: (i, 0)),
                  pl.BlockSpec((D_MODEL, tn), lambda i, j: (0, j))],
        out_specs=[pl.BlockSpec((tm, tn), lambda i, j: (i, jnp.minimum(j, 1))),
                   pl.BlockSpec((tm, tn), lambda i, j: (i, jnp.maximum(j - 2, 0)))],
        out_shape=[jax.ShapeDtypeStruct((n, 2 * LRU_WIDTH), F32),
                   jax.ShapeDtypeStruct((n, 2048), BF16)],
        scratch_shapes=[pltpu.VMEM((tm, D_MODEL), BF16)],
        compiler_params=_params("parallel", "arbitrary"),
        name="in_proj",
    )(x, w)


def _matmul_kernel(x_ref, w_ref, o_ref):
    o_ref[...] = jnp.dot(x_ref[...].astype(BF16), w_ref[...],
                         preferred_element_type=F32).astype(o_ref.dtype)


def _matmul(x, w, out_dtype):
    m, k = x.shape
    n = w.shape[1]
    tm = min(512, m)
    return pl.pallas_call(
        _matmul_kernel,
        grid=(m // tm,),
        in_specs=[pl.BlockSpec((tm, k), lambda i: (i, 0)),
                  pl.BlockSpec((k, n), lambda i: (0, 0))],
        out_specs=pl.BlockSpec((tm, n), lambda i: (i, 0)),
        out_shape=jax.ShapeDtypeStruct((m, n), out_dtype),
        compiler_params=_params("parallel"),
        name="mem_kv_proj",
    )(x, w)


LRU_CHUNK = 512
LRU_SUB = 128
HALO = SUBLANES


def _gelu_tanh(x):
    return 0.5 * x * (1.0 + jnp.tanh(0.7978845608028654 * (x + 0.044715 * (x * x * x))))


def _lru_kernel(*refs, reverse, seq_len, n_chunks):
    if reverse:
        (x_ref, prev_ref, next_ref, cw_ref, cb_ref, wbd_ref, gb_ref, lam_ref,
         hf_ref, gl_ref, gain_ref, o_ref, xs_ref, a_ref, u_ref, carry_ref) = refs
    else:
        (x_ref, prev_ref, next_ref, cw_ref, cb_ref, wbd_ref, gb_ref, lam_ref,
         o_ref, xs_ref, a_ref, u_ref, carry_ref) = refs

    step = pl.program_id(0)
    chunk = (n_chunks - 1 - step) if reverse else step
    pos = (chunk * LRU_CHUNK) % seq_len
    seq_first = pos == 0
    seq_last = pos == seq_len - LRU_CHUNK

    xs_ref[0:HALO, :] = jnp.where(seq_first, 0.0, prev_ref[...])
    xs_ref[HALO:HALO + LRU_CHUNK, :] = x_ref[...]
    xs_ref[HALO + LRU_CHUNK:, :] = jnp.where(seq_last, 0.0, next_ref[...])

    @pl.when(seq_last if reverse else seq_first)
    def _():
        carry_ref[...] = jnp.zeros_like(carry_ref)

    lam = lam_ref[...]
    neg = -lam
    softplus = jnp.maximum(neg, 0.0) + jnp.log(1.0 + jnp.exp(-jnp.abs(neg)))
    a_scale = -LRU_C * softplus
    cw = cw_ref[...]
    cb = cb_ref[...]
    win = LRU_SUB + 2 * HALO

    def sub_block(s, _):
        r0 = pl.multiple_of(s * LRU_SUB, LRU_SUB)
        xw = xs_ref[pl.ds(r0, win), :]
        xc = (cw[0:1] * pltpu.roll(xw, 2, 0) + cw[1:2] * pltpu.roll(xw, 1, 0)
              + cw[2:3] * xw + cw[3:4] * pltpu.roll(xw, win - 1, 0))
        xc = xc[HALO:HALO + LRU_SUB] + cb
        xcb = xc.astype(BF16)
        for g in range(N_GATE_GROUPS):
            lo = g * GATE_GROUP
            gates = jnp.dot(xcb[:, lo:lo + GATE_GROUP], wbd_ref[g],
                            preferred_element_type=F32) + gb_ref[g]
            r = jax.nn.sigmoid(gates[:, :GATE_GROUP])
            i = jax.nn.sigmoid(gates[:, GATE_GROUP:])
            log_a = a_scale[:, lo:lo + GATE_GROUP] * r
            a = jnp.exp(log_a)
            mult = jnp.sqrt(1.0 - jnp.exp(2.0 * log_a))
            a_ref[pl.ds(r0, LRU_SUB), lo:lo + GATE_GROUP] = a
            u_ref[pl.ds(r0, LRU_SUB), lo:lo + GATE_GROUP] = mult * (i * xc[:, lo:lo + GATE_GROUP])
        return 0

    lax.fori_loop(0, LRU_CHUNK // LRU_SUB, sub_block, 0)

    row = lax.broadcasted_iota(jnp.int32, (SUBLANES, LRU_WIDTH), 0)
    n_blocks = LRU_CHUNK // SUBLANES

    def scan_block(k, carry):
        blk = (n_blocks - 1 - k) if reverse else k
        r0 = pl.multiple_of(blk * SUBLANES, SUBLANES)
        a = a_ref[pl.ds(r0, SUBLANES), :]
        u = u_ref[pl.ds(r0, SUBLANES), :]
        for d in (1, 2, 4):
            if reverse:
                keep = row < SUBLANES - d
                shift = SUBLANES - d
            else:
                keep = row >= d
                shift = d
            u = u + a * jnp.where(keep, pltpu.roll(u, shift, 0), 0.0)
            a = a * jnp.where(keep, pltpu.roll(a, shift, 0), 1.0)
        h = u + a * carry
        u_ref[pl.ds(r0, SUBLANES), :] = h
        edge = h[0:1, :] if reverse else h[SUBLANES - 1:SUBLANES, :]
        return jnp.broadcast_to(edge, (SUBLANES, LRU_WIDTH))

    carry_ref[...] = lax.fori_loop(0, n_blocks, scan_block, carry_ref[...])

    if not reverse:
        o_ref[...] = u_ref[...]
    else:
        gain = gain_ref[...]

        def out_block(s, _):
            r0 = pl.multiple_of(s * LRU_SUB, LRU_SUB)
            h = u_ref[pl.ds(r0, LRU_SUB), :] + hf_ref[pl.ds(r0, LRU_SUB), :]
            y = h * _gelu_tanh(gl_ref[pl.ds(r0, LRU_SUB), :])
            o_ref[pl.ds(r0, LRU_SUB), :] = _rms_gain_rows(y, gain).astype(BF16)
            return 0

        lax.fori_loop(0, LRU_CHUNK // LRU_SUB, out_block, 0)


def _lru_direction(u_lru, conv_w, conv_b, wbd, gate_b, lam, seq_len, reverse,
                   h_fwd=None, gain=None):
    n = u_lru.shape[0]
    n_chunks = n // LRU_CHUNK
    halo_per_chunk = LRU_CHUNK // HALO
    n_halo_blocks = n // HALO

    def cidx(s):
        return (n_chunks - 1 - s) if reverse else s

    def full(shape):
        return pl.BlockSpec(shape, lambda s: (0,) * len(shape))

    in_specs = [
        pl.BlockSpec((LRU_CHUNK, LRU_WIDTH), lambda s: (cidx(s), 0)),
        pl.BlockSpec((HALO, LRU_WIDTH),
                     lambda s: (jnp.maximum(cidx(s) * halo_per_chunk - 1, 0), 0)),
        pl.BlockSpec((HALO, LRU_WIDTH),
                     lambda s: (jnp.minimum((cidx(s) + 1) * halo_per_chunk, n_halo_blocks - 1), 0)),
        full((4, LRU_WIDTH)), full((1, LRU_WIDTH)),
        full((N_GATE_GROUPS, GATE_GROUP, 2 * GATE_GROUP)),
        full((N_GATE_GROUPS, 1, 2 * GATE_GROUP)),
        full((1, LRU_WIDTH)),
    ]
    args = [u_lru, u_lru, u_lru, conv_w, conv_b, wbd, gate_b, lam]
    if reverse:
        in_specs += [pl.BlockSpec((LRU_CHUNK, LRU_WIDTH), lambda s: (cidx(s), 0)),
                     pl.BlockSpec((LRU_CHUNK, LRU_WIDTH), lambda s: (cidx(s), 1)),
                     full((1, LRU_WIDTH))]
        args += [h_fwd, u_lru, gain]
    out_dtype = BF16 if reverse else F32
    return pl.pallas_call(
        functools.partial(_lru_kernel, reverse=reverse, seq_len=seq_len, n_chunks=n_chunks),
        grid=(n_chunks,),
        in_specs=in_specs,
        out_specs=pl.BlockSpec((LRU_CHUNK, LRU_WIDTH), lambda s: (cidx(s), 0)),
        out_shape=jax.ShapeDtypeStruct((n, LRU_WIDTH), out_dtype),
        scratch_shapes=[pltpu.VMEM((LRU_CHUNK + 2 * HALO, LRU_WIDTH), F32),
                        pltpu.VMEM((LRU_CHUNK, LRU_WIDTH), F32),
                        pltpu.VMEM((LRU_CHUNK, LRU_WIDTH), F32),
                        pltpu.VMEM((SUBLANES, LRU_WIDTH), F32)],
        compiler_params=_params("arbitrary"),
        name="lru_bwd" if reverse else "lru_fwd",
    )(*args)


NA_QROWS = 8
NA_KROWS = 16
NA_KBLK = 4
PAIR = 2 * NA_HEAD_DIM


def _natten_kernel(q_ref, k0, k1, k2, k3, v0, v1, v2, v3, bias_ref, gain_ref, o_ref,
                   ks_ref, vs_ref, *, rows):
    rblk = pl.program_id(1)
    for j, (kr, vr) in enumerate(((k0, v0), (k1, v1), (k2, v2), (k3, v3))):
        ks_ref[NA_KBLK * j:NA_KBLK * (j + 1)] = kr[0]
        vs_ref[NA_KBLK * j:NA_KBLK * (j + 1)] = vr[0]
    key_start = jnp.clip(rblk * NA_QROWS - NA_WIN_ROWS // 2, 0, rows - NA_KROWS)
    lane = lax.broadcasted_iota(jnp.int32, (GRID_W, PAIR), 1)
    lower = lane < NA_HEAD_DIM
    scale = NA_HEAD_DIM ** -0.5
    gain = gain_ref[...]

    def one_row(qr, _):
        r = rblk * NA_QROWS + qr
        win_start = jnp.clip(r - NA_WIN_ROWS // 2, 0, rows - NA_WIN_ROWS)
        d0 = win_start - key_start
        off = r - win_start
        q = q_ref[0, qr]
        outs = []
        for p in range(NA_HEADS // 2):
            qp = q[:, p * PAIR:(p + 1) * PAIR] * scale
            q2 = jnp.concatenate([jnp.where(lower, qp, 0.0), jnp.where(lower, 0.0, qp)],
                                 axis=0).astype(BF16)
            kp = ks_ref[pl.ds(d0, NA_WIN_ROWS), :, p * PAIR:(p + 1) * PAIR]
            vp = vs_ref[pl.ds(d0, NA_WIN_ROWS), :, p * PAIR:(p + 1) * PAIR]
            kp = kp.reshape(NA_WIN_ROWS * GRID_W, PAIR)
            vp = vp.reshape(NA_WIN_ROWS * GRID_W, PAIR)
            s = lax.dot_general(q2, kp, (((1,), (1,)), ((), ())),
                                preferred_element_type=F32)
            s = s + bias_ref[off, p]
            m = jnp.max(s, axis=-1, keepdims=True)
            e = jnp.exp(s - m)
            l = jnp.sum(e, axis=-1, keepdims=True)
            o = jnp.dot(e.astype(BF16), vp, preferred_element_type=F32) / l
            outs.append(jnp.where(lower, o[:GRID_W], o[GRID_W:]))
        y = jnp.concatenate(outs, axis=-1)
        o_ref[0, qr] = _rms_gain_rows(y, gain).astype(BF16)
        return 0

    lax.fori_loop(0, NA_QROWS, one_row, 0)


def _natten(u_att4, bias, gain):
    b, rows = u_att4.shape[0], u_att4.shape[1]
    n_kblk = rows // NA_KBLK

    def kv_spec(j, lane_block):
        def imap(bi, ri):
            start = jnp.clip(2 * ri - 1, 0, n_kblk - NA_KROWS // NA_KBLK)
            return (bi, start + j, 0, lane_block)
        return pl.BlockSpec((1, NA_KBLK, GRID_W, NA_WIDTH), imap)

    in_specs = ([pl.BlockSpec((1, NA_QROWS, GRID_W, NA_WIDTH), lambda bi, ri: (bi, ri, 0, 0))]
                + [kv_spec(j, 1) for j in range(4)] + [kv_spec(j, 2) for j in range(4)]
                + [pl.BlockSpec(bias.shape, lambda bi, ri: (0, 0, 0, 0)),
                   pl.BlockSpec((1, NA_WIDTH), lambda bi, ri: (0, 0))])
    return pl.pallas_call(
        functools.partial(_natten_kernel, rows=rows),
        grid=(b, rows // NA_QROWS),
        in_specs=in_specs,
        out_specs=pl.BlockSpec((1, NA_QROWS, GRID_W, NA_WIDTH), lambda bi, ri: (bi, ri, 0, 0)),
        out_shape=jax.ShapeDtypeStruct((b, rows, GRID_W, NA_WIDTH), BF16),
        scratch_shapes=[pltpu.VMEM((NA_KROWS, GRID_W, NA_WIDTH), BF16),
                        pltpu.VMEM((NA_KROWS, GRID_W, NA_WIDTH), BF16)],
        compiler_params=_params("parallel", "parallel"),
        name="natten",
    )(*([u_att4] * 9), bias, gain)


def _natten_bias_table(rpb):
    off = np.arange(NA_WIN_ROWS)
    kr = np.arange(NA_WIN_ROWS)
    dr = kr[None, :] - off[:, None] + (NA_WIN_ROWS - 1)
    row_sel = (dr[:, :, None] == np.arange(2 * NA_WIN_ROWS - 1)).astype(np.float32)
    qc = np.arange(GRID_W)
    kc = np.arange(GRID_W)
    start = np.clip(qc - NA_WIN_COLS // 2, 0, GRID_W - NA_WIN_COLS)
    valid = (kc[None, :] >= start[:, None]) & (kc[None, :] < start[:, None] + NA_WIN_COLS)
    dc = kc[None, :] - qc[:, None] + (NA_WIN_COLS - 1)
    col_sel = ((dc[:, :, None] == np.arange(2 * NA_WIN_COLS - 1)) & valid[:, :, None])
    col_sel = col_sel.astype(np.float32)
    t = jnp.einsum('oka,hac,qjc->ohqkj', jnp.asarray(row_sel), rpb.astype(F32),
                   jnp.asarray(col_sel), precision=lax.Precision.HIGHEST)
    t = jnp.where(jnp.asarray(valid)[None, None, :, None, :], t, MASK_VALUE)
    return t.reshape(NA_WIN_ROWS, NA_HEADS // 2, 2 * GRID_W, NA_WIN_ROWS * GRID_W)


MEM_TILE = 512


def _mem_attn_kernel(q_ref, kv_ref, gain_ref, o_ref):
    q = q_ref[0]
    kv = kv_ref[0]
    scale = MEM_HEAD_DIM ** -0.5
    outs = []
    for h in range(MEM_HEADS):
        lo = h * MEM_HEAD_DIM
        s = lax.dot_general(q[:, lo:lo + MEM_HEAD_DIM], kv[:, lo:lo + MEM_HEAD_DIM],
                            (((1,), (1,)), ((), ())), preferred_element_type=F32) * scale
        m = jnp.max(s, axis=-1, keepdims=True)
        e = jnp.exp(s - m)
        l = jnp.sum(e, axis=-1, keepdims=True)
        v = kv[:, MEM_WIDTH + lo:MEM_WIDTH + lo + MEM_HEAD_DIM]
        outs.append(jnp.dot(e.astype(BF16), v, preferred_element_type=F32) / l)
    y = jnp.concatenate(outs, axis=-1)
    o_ref[0] = _rms_gain_rows(y, gain_ref[...]).astype(BF16)


def _mem_attn(u_att3, kv, gain):
    b, t = u_att3.shape[0], u_att3.shape[1]
    tm = min(MEM_TILE, t)
    return pl.pallas_call(
        _mem_attn_kernel,
        grid=(b, t // tm),
        in_specs=[pl.BlockSpec((1, tm, MEM_WIDTH), lambda bi, ti: (bi, ti, 3)),
                  pl.BlockSpec((1, N_MEM, 2 * MEM_WIDTH), lambda bi, ti: (bi, 0, 0)),
                  pl.BlockSpec((1, MEM_WIDTH), lambda bi, ti: (0, 0))],
        out_specs=pl.BlockSpec((1, tm, MEM_WIDTH), lambda bi, ti: (bi, ti, 0)),
        out_shape=jax.ShapeDtypeStruct((b, t, MEM_WIDTH), BF16),
        compiler_params=_params("parallel", "parallel"),
        name="mem_attn",
    )(u_att3, kv, gain)


OUT_TILE = 512


def _out_proj_kernel(yl_ref, yn_ref, ym_ref, w_ref, x_ref, g_ref, b_ref, o_ref):
    acc = jnp.dot(yl_ref[...], w_ref[0:LRU_WIDTH, :], preferred_element_type=F32)
    acc += jnp.dot(yn_ref[...], w_ref[LRU_WIDTH:LRU_WIDTH + NA_WIDTH, :],
                   preferred_element_type=F32)
    acc += jnp.dot(ym_ref[...], w_ref[LRU_WIDTH + NA_WIDTH:, :], preferred_element_type=F32)
    z = DN_ALPHA * x_ref[...] + acc
    o_ref[...] = _layer_norm_rows(z, g_ref[...], b_ref[...])


def _out_proj(y_lru, y_na, y_mem, w, x, g, b):
    n = x.shape[0]
    tm = min(OUT_TILE, n)
    row = lambda width: pl.BlockSpec((tm, width), lambda i: (i, 0))
    vec = pl.BlockSpec((1, D_MODEL), lambda i: (0, 0))
    return pl.pallas_call(
        _out_proj_kernel,
        grid=(n // tm,),
        in_specs=[row(LRU_WIDTH), row(NA_WIDTH), row(MEM_WIDTH),
                  pl.BlockSpec((D_MODEL, D_MODEL), lambda i: (0, 0)),
                  row(D_MODEL), vec, vec],
        out_specs=row(D_MODEL),
        out_shape=jax.ShapeDtypeStruct((n, D_MODEL), F32),
        compiler_params=_params("parallel"),
        name="out_proj_ln",
    )(y_lru, y_na, y_mem, w, x, g, b)


FFN_TILE = 512
FFN_CHUNK = 512


def _ffn_kernel(x_ref, w1_ref, w3_ref, w2_ref, g_ref, b_ref, o_ref, xb_ref, acc_ref):
    j = pl.program_id(1)

    @pl.when(j == 0)
    def _():
        xb_ref[...] = x_ref[...].astype(BF16)
        acc_ref[...] = jnp.zeros_like(acc_ref)

    xb = xb_ref[...]
    h1 = jnp.dot(xb, w1_ref[...], preferred_element_type=F32)
    h3 = jnp.dot(xb, w3_ref[...], preferred_element_type=F32)
    h = (h1 * jax.nn.sigmoid(h1) * h3).astype(BF16)
    acc_ref[...] += jnp.dot(h, w2_ref[...], preferred_element_type=F32)

    @pl.when(j == pl.num_programs(1) - 1)
    def _():
        z = DN_ALPHA * x_ref[...] + acc_ref[...]
        o_ref[...] = _layer_norm_rows(z, g_ref[...], b_ref[...])


def _ffn(x, w1, w3, w2, g, b):
    n = x.shape[0]
    tm = min(FFN_TILE, n)
    fc = FFN_CHUNK
    vec = pl.BlockSpec((1, D_MODEL), lambda i, j: (0, 0))
    return pl.pallas_call(
        _ffn_kernel,
        grid=(n // tm, D_FF // fc),
        in_specs=[pl.BlockSpec((tm, D_MODEL), lambda i, j: (i, 0)),
                  pl.BlockSpec((D_MODEL, fc), lambda i, j: (0, j)),
                  pl.BlockSpec((D_MODEL, fc), lambda i, j: (0, j)),
                  pl.BlockSpec((fc, D_MODEL), lambda i, j: (j, 0)),
                  vec, vec],
        out_specs=pl.BlockSpec((tm, D_MODEL), lambda i, j: (i, 0)),
        out_shape=jax.ShapeDtypeStruct((n, D_MODEL), F32),
        scratch_shapes=[pltpu.VMEM((tm, D_MODEL), BF16), pltpu.VMEM((tm, D_MODEL), F32)],
        compiler_params=_params("parallel", "arbitrary"),
        name="ffn_ln",
    )(x, w1, w3, w2, g, b)


MOE_TILE = 512
MOE_CHUNK = 512
GATE_LANES = 128


def _router_gates(x, rw):
    logits = jnp.dot(x, rw, preferred_element_type=F32, precision=lax.Precision.HIGHEST)
    lane = lax.broadcasted_iota(jnp.int32, logits.shape, 1)
    logits = jnp.where(lane < N_EXPERTS, logits, -jnp.inf)
    m1 = jnp.max(logits, axis=-1, keepdims=True)
    i1 = jnp.min(jnp.where(logits == m1, lane, GATE_LANES), axis=-1, keepdims=True)
    rest = jnp.where(lane == i1, -jnp.inf, logits)
    m2 = jnp.max(rest, axis=-1, keepdims=True)
    i2 = jnp.min(jnp.where(rest == m2, lane, GATE_LANES), axis=-1, keepdims=True)
    e2 = jnp.exp(m2 - m1)
    p1 = 1.0 / (1.0 + e2)
    p2 = e2 / (1.0 + e2)
    return jnp.where(lane == i1, p1, 0.0) + jnp.where(lane == i2, p2, 0.0)


def _moe_kernel(x_ref, rw_ref, w1_ref, w3_ref, w2_ref, g_ref, b_ref, o_ref,
                xb_ref, acc_ref, gates_ref):
    e = pl.program_id(1)
    j = pl.program_id(2)
    first = jnp.logical_and(e == 0, j == 0)
    last = jnp.logical_and(e == pl.num_programs(1) - 1, j == pl.num_programs(2) - 1)

    @pl.when(first)
    def _():
        x = x_ref[...]
        xb_ref[...] = x.astype(BF16)
        acc_ref[...] = jnp.zeros_like(acc_ref)
        gates_ref[...] = _router_gates(x, rw_ref[...])

    xb = xb_ref[...]
    h1 = jnp.dot(xb, w1_ref[0], preferred_element_type=F32)
    h3 = jnp.dot(xb, w3_ref[0], preferred_element_type=F32)
    h = (h1 * jax.nn.sigmoid(h1) * h3).astype(BF16)
    gates = gates_ref[...]
    lane = lax.broadcasted_iota(jnp.int32, gates.shape, 1)
    gate_e = jnp.sum(jnp.where(lane == e, gates, 0.0), axis=-1, keepdims=True)
    acc_ref[...] += gate_e * jnp.dot(h, w2_ref[0], preferred_element_type=F32)

    @pl.when(last)
    def _():
        z = DN_ALPHA * x_ref[...] + acc_ref[...]
        o_ref[...] = _layer_norm_rows(z, g_ref[...], b_ref[...])


def _moe(x, rw, w1, w3, w2, g, b):
    n = x.shape[0]
    tm = min(MOE_TILE, n)
    fc = MOE_CHUNK
    vec = pl.BlockSpec((1, D_MODEL), lambda i, e, j: (0, 0))
    return pl.pallas_call(
        _moe_kernel,
        grid=(n // tm, N_EXPERTS, D_FF_EXPERT // fc),
        in_specs=[pl.BlockSpec((tm, D_MODEL), lambda i, e, j: (i, 0)),
                  pl.BlockSpec((D_MODEL, GATE_LANES), lambda i, e, j: (0, 0)),
                  pl.BlockSpec((1, D_MODEL, fc), lambda i, e, j: (e, 0, j)),
                  pl.BlockSpec((1, D_MODEL, fc), lambda i, e, j: (e, 0, j)),
                  pl.BlockSpec((1, fc, D_MODEL), lambda i, e, j: (e, j, 0)),
                  vec, vec],
        out_specs=pl.BlockSpec((tm, D_MODEL), lambda i, e, j: (i, 0)),
        out_shape=jax.ShapeDtypeStruct((n, D_MODEL), F32),
        scratch_shapes=[pltpu.VMEM((tm, D_MODEL), BF16), pltpu.VMEM((tm, D_MODEL), F32),
                        pltpu.VMEM((tm, GATE_LANES), F32)],
        compiler_params=_params("parallel", "arbitrary", "arbitrary"),
        name="moe_ln",
    )(x, rw, w1, w3, w2, g, b)


def _block_diag_gates(gate_w, gate_b):
    per = GATE_GROUP // LRU_BLOCK
    w = gate_w.reshape(2, N_GATE_GROUPS, per, LRU_BLOCK, LRU_BLOCK)
    eye = jnp.eye(per, dtype=gate_w.dtype)
    wbd = w[:, :, :, :, None, :] * eye[None, None, :, None, :, None]
    wbd = jnp.transpose(wbd, (1, 2, 3, 0, 4, 5)).reshape(N_GATE_GROUPS, GATE_GROUP, 2 * GATE_GROUP)
    b = gate_b.reshape(2, N_GATE_GROUPS, GATE_GROUP)
    b = jnp.transpose(b, (1, 0, 2)).reshape(N_GATE_GROUPS, 1, 2 * GATE_GROUP)
    return wbd.astype(BF16), b.astype(F32)


def _prepare_layer(l, w_in, conv_w, conv_b, lru_gate_w, lru_gate_b, lru_lambda, rpb, w_mem_kv,
                   mix_gain, w_out, ln1_g, ln1_b, ffn_w1, ffn_w3, ffn_w2, router_w, moe_w1,
                   moe_w3, moe_w2, ln2_g, ln2_b):
    p = {}
    p['w_in'] = w_in[l].astype(BF16)
    p['conv_w'] = conv_w[l].astype(F32)
    p['conv_b'] = conv_b[l].astype(F32).reshape(1, LRU_WIDTH)
    p['gates'] = [_block_diag_gates(lru_gate_w[l, d], lru_gate_b[l, d]) for d in range(2)]
    p['lam'] = [lru_lambda[l, d].astype(F32).reshape(1, LRU_WIDTH) for d in range(2)]
    p['na_bias'] = _natten_bias_table(rpb[l])
    p['w_mem_kv'] = w_mem_kv[l].astype(BF16)
    gain = mix_gain[l].astype(F32)
    p['gain_lru'] = gain[:LRU_WIDTH].reshape(1, LRU_WIDTH)
    p['gain_na'] = gain[LRU_WIDTH:LRU_WIDTH + NA_WIDTH].reshape(1, NA_WIDTH)
    p['gain_mem'] = gain[LRU_WIDTH + NA_WIDTH:].reshape(1, MEM_WIDTH)
    p['w_out'] = w_out[l].astype(BF16)
    vec = lambda v: v.astype(F32).reshape(1, D_MODEL)
    p['ln1'] = (vec(ln1_g[l]), vec(ln1_b[l]))
    p['ln2'] = (vec(ln2_g[l]), vec(ln2_b[l]))
    j = l // 2
    if l % 2 == 0:
        p['ffn'] = (ffn_w1[j].astype(BF16), ffn_w3[j].astype(BF16), ffn_w2[j].astype(BF16))
    else:
        rw = jnp.zeros((D_MODEL, GATE_LANES), F32).at[:, :N_EXPERTS].set(router_w[j].astype(F32))
        p['moe'] = (rw, moe_w1[j].astype(BF16), moe_w3[j].astype(BF16), moe_w2[j].astype(BF16))
    return p


def _trunk(x, mem, layers):
    b, t, _ = x.shape
    n = b * t
    rows = t // GRID_W
    x = x.reshape(n, D_MODEL)
    mem2 = mem.reshape(b * N_MEM, D_MODEL)
    for p in layers:
        u_lru, u_att = _in_proj(x, p['w_in'])
        (wbd_f, gb_f), (wbd_b, gb_b) = p['gates']
        h_fwd = _lru_direction(u_lru, p['conv_w'], p['conv_b'], wbd_f, gb_f, p['lam'][0], t, False)
        y_lru = _lru_direction(u_lru, p['conv_w'], p['conv_b'], wbd_b, gb_b, p['lam'][1], t, True,
                               h_fwd=h_fwd, gain=p['gain_lru'])
        y_na = _natten(u_att.reshape(b, rows, GRID_W, 2048), p['na_bias'], p['gain_na'])
        kv = _matmul(mem2, p['w_mem_kv'], BF16).reshape(b, N_MEM, 2 * MEM_WIDTH)
        y_mem = _mem_attn(u_att.reshape(b, t, 2048), kv, p['gain_mem'])
        x = _out_proj(y_lru, y_na.reshape(n, NA_WIDTH), y_mem.reshape(n, MEM_WIDTH),
                      p['w_out'], x, *p['ln1'])
        if 'ffn' in p:
            x = _ffn(x, *p['ffn'], *p['ln2'])
        else:
            x = _moe(x, *p['moe'], *p['ln2'])
    return x.reshape(b, t, D_MODEL)


def kernel(x_prompt, x_sample, mem_prompt, mem_sample, w_in, conv_w, conv_b, lru_gate_w,
           lru_gate_b, lru_lambda, rpb, w_mem_kv, mix_gain, w_out, ln1_g, ln1_b, ffn_w1,
           ffn_w3, ffn_w2, router_w, moe_w1, moe_w3, moe_w2, ln2_g, ln2_b):
    weights = (w_in, conv_w, conv_b, lru_gate_w, lru_gate_b, lru_lambda, rpb, w_mem_kv,
               mix_gain, w_out, ln1_g, ln1_b, ffn_w1, ffn_w3, ffn_w2, router_w, moe_w1,
               moe_w3, moe_w2, ln2_g, ln2_b)
    layers = [_prepare_layer(l, *weights) for l in range(DEPTH)]
    return (_trunk(x_prompt, mem_prompt, layers), _trunk(x_sample, mem_sample, layers))
```

```python
import functools

import numpy as np
import jax
import jax.numpy as jnp
from jax import lax
from jax.experimental import pallas as pl
from jax.experimental.pallas import tpu as pltpu

F32 = jnp.float32
BF16 = jnp.bfloat16

D_MODEL = 2048
DEPTH = 4
GRID_W = 64
LRU_WIDTH = 1024
LRU_BLOCKS = 16
LRU_BLOCK = 64
LRU_C = 8.0
NA_WIDTH = 512
NA_HEADS = 8
NA_HEAD_DIM = 64
NA_WIN_ROWS = 8
NA_WIN_COLS = 16
MASK_VALUE = -1e30
MEM_WIDTH = 512
MEM_HEADS = 4
MEM_HEAD_DIM = 128
N_MEM = 256
IN_WIDTH = 4096
D_FF = 4096
N_EXPERTS = 8
D_FF_EXPERT = 1024
DN_ALPHA = (2 * DEPTH) ** 0.25
LN_EPS = 1e-5
RMS_EPS = 1e-6

VMEM_LIMIT_BYTES = 56 * 1024 * 1024
GATE_GROUP = 256
N_GATE_GROUPS = LRU_WIDTH // GATE_GROUP
SUBLANES = 8


def _params(*semantics):
    return pltpu.CompilerParams(dimension_semantics=semantics,
                                vmem_limit_bytes=VMEM_LIMIT_BYTES)


def _layer_norm_rows(z, g, b):
    mu = jnp.mean(z, axis=-1, keepdims=True)
    zc = z - mu
    var = jnp.mean(zc * zc, axis=-1, keepdims=True)
    return zc * lax.rsqrt(var + LN_EPS) * g + b


def _rms_gain_rows(y, gain):
    ms = jnp.mean(y * y, axis=-1, keepdims=True)
    return y * lax.rsqrt(ms + RMS_EPS) * gain


def _in_proj_kernel(x_ref, w_ref, lru_ref, att_ref, xb_ref):
    j = pl.program_id(1)

    @pl.when(j == 0)
    def _():
        xb_ref[...] = x_ref[...].astype(BF16)

    y = jnp.dot(xb_ref[...], w_ref[...], preferred_element_type=F32)

    @pl.when(j < 2)
    def _():
        lru_ref[...] = y

    @pl.when(j >= 2)
    def _():
        att_ref[...] = y.astype(BF16)


def _in_proj(x, w):
    n = x.shape[0]
    tm = min(1024, n)
    tn = 1024
    return pl.pallas_call(
        _in_proj_kernel,
        grid=(n // tm, IN_WIDTH // tn),
        in_specs=[pl.BlockSpec((tm, D_MODEL), lambda i, j: (i, 0)),
                  pl.BlockSpec((D_MODEL, tn), lambda i, j: (0, j))],
        out_specs=[pl.BlockSpec((tm, tn), lambda i, j: (i, jnp.minimum(j, 1))),
                   pl.BlockSpec((tm, tn), lambda i, j: (i, jnp.maximum(j - 2, 0)))],
        out_shape=[jax.ShapeDtypeStruct((n, 2 * LRU_WIDTH), F32),
                   jax.ShapeDtypeStruct((n, 2048), BF16)],
        scratch_shapes=[pltpu.VMEM((tm, D_MODEL), BF16)],
        compiler_params=_params("parallel", "arbitrary"),
        name="in_proj",
    )(x, w)


def _matmul_kernel(x_ref, w_ref, o_ref):
    o_ref[...] = jnp.dot(x_ref[...].astype(BF16), w_ref[...],
                         preferred_element_type=F32).astype(o_ref.dtype)


def _matmul(x, w, out_dtype):
    m, k = x.shape
    n = w.shape[1]
    tm = min(512, m)
    return pl.pallas_call(
        _matmul_kernel,
        grid=(m // tm,),
        in_specs=[pl.BlockSpec((tm, k), lambda i: (i, 0)),
                  pl.BlockSpec((k, n), lambda i: (0, 0))],
        out_specs=pl.BlockSpec((tm, n), lambda i: (i, 0)),
        out_shape=jax.ShapeDtypeStruct((m, n), out_dtype),
        compiler_params=_params("parallel"),
        name="mem_kv_proj",
    )(x, w)


LRU_CHUNK = 512
LRU_SUB = 128
HALO = SUBLANES


def _sigmoid(x):
    return 0.5 * jnp.tanh(0.5 * x) + 0.5


def _gelu_tanh(x):
    return 0.5 * x * (1.0 + jnp.tanh(0.7978845608028654 * (x + 0.044715 * (x * x * x))))


def _lru_kernel(*refs, reverse, seq_len, n_chunks):
    if reverse:
        (x_ref, prev_ref, next_ref, cw_ref, cb_ref, wbd_ref, gb_ref, lam_ref,
         hf_ref, gl_ref, gain_ref, o_ref, xs_ref, a_ref, u_ref, carry_ref) = refs
    else:
        (x_ref, prev_ref, next_ref, cw_ref, cb_ref, wbd_ref, gb_ref, lam_ref,
         o_ref, xs_ref, a_ref, u_ref, carry_ref) = refs

    step = pl.program_id(0)
    chunk = (n_chunks - 1 - step) if reverse else step
    pos = (chunk * LRU_CHUNK) % seq_len
    seq_first = pos == 0
    seq_last = pos == seq_len - LRU_CHUNK

    xs_ref[0:HALO, :] = jnp.where(seq_first, 0.0, prev_ref[...])
    xs_ref[HALO:HALO + LRU_CHUNK, :] = x_ref[...]
    xs_ref[HALO + LRU_CHUNK:, :] = jnp.where(seq_last, 0.0, next_ref[...])

    @pl.when(seq_last if reverse else seq_first)
    def _():
        carry_ref[...] = jnp.zeros_like(carry_ref)

    lam = lam_ref[...]
    neg = -lam
    softplus = jnp.maximum(neg, 0.0) + jnp.log(1.0 + jnp.exp(-jnp.abs(neg)))
    a_scale = -LRU_C * softplus
    cw = cw_ref[...]
    cb = cb_ref[...]
    win = LRU_SUB + 2 * HALO

    def sub_block(s, _):
        r0 = pl.multiple_of(s * LRU_SUB, LRU_SUB)
        xw = xs_ref[pl.ds(r0, win), :]
        xc = (cw[0:1] * pltpu.roll(xw, 2, 0) + cw[1:2] * pltpu.roll(xw, 1, 0)
              + cw[2:3] * xw + cw[3:4] * pltpu.roll(xw, win - 1, 0))
        xc = xc[HALO:HALO + LRU_SUB] + cb
        xcb = xc.astype(BF16)
        for g in range(N_GATE_GROUPS):
            lo = g * GATE_GROUP
            gates = jnp.dot(xcb[:, lo:lo + GATE_GROUP], wbd_ref[g],
                            preferred_element_type=F32) + gb_ref[g]
            r = _sigmoid(gates[:, :GATE_GROUP])
            i = _sigmoid(gates[:, GATE_GROUP:])
            a = jnp.exp(a_scale[:, lo:lo + GATE_GROUP] * r)
            mult = jnp.sqrt(1.0 - a * a)
            a_ref[pl.ds(r0, LRU_SUB), lo:lo + GATE_GROUP] = a
            u_ref[pl.ds(r0, LRU_SUB), lo:lo + GATE_GROUP] = mult * (i * xc[:, lo:lo + GATE_GROUP])
        return 0

    lax.fori_loop(0, LRU_CHUNK // LRU_SUB, sub_block, 0)

    row = lax.broadcasted_iota(jnp.int32, (SUBLANES, LRU_WIDTH), 0)
    n_blocks = LRU_CHUNK // SUBLANES

    def scan_block(k, carry):
        blk = (n_blocks - 1 - k) if reverse else k
        r0 = pl.multiple_of(blk * SUBLANES, SUBLANES)
        a = a_ref[pl.ds(r0, SUBLANES), :]
        u = u_ref[pl.ds(r0, SUBLANES), :]
        for d in (1, 2, 4):
            if reverse:
                keep = row < SUBLANES - d
                shift = SUBLANES - d
            else:
                keep = row >= d
                shift = d
            u = u + a * jnp.where(keep, pltpu.roll(u, shift, 0), 0.0)
            a = a * jnp.where(keep, pltpu.roll(a, shift, 0), 1.0)
        h = u + a * carry
        u_ref[pl.ds(r0, SUBLANES), :] = h
        edge = h[0:1, :] if reverse else h[SUBLANES - 1:SUBLANES, :]
        return jnp.broadcast_to(edge, (SUBLANES, LRU_WIDTH))

    carry_ref[...] = lax.fori_loop(0, n_blocks, scan_block, carry_ref[...])

    if not reverse:
        o_ref[...] = u_ref[...]
    else:
        gain = gain_ref[...]

        def out_block(s, _):
            r0 = pl.multiple_of(s * LRU_SUB, LRU_SUB)
            h = u_ref[pl.ds(r0, LRU_SUB), :] + hf_ref[pl.ds(r0, LRU_SUB), :]
            y = h * _gelu_tanh(gl_ref[pl.ds(r0, LRU_SUB), :])
            o_ref[pl.ds(r0, LRU_SUB), :] = _rms_gain_rows(y, gain).astype(BF16)
            return 0

        lax.fori_loop(0, LRU_CHUNK // LRU_SUB, out_block, 0)


def _lru_direction(u_lru, conv_w, conv_b, wbd, gate_b, lam, seq_len, reverse,
                   h_fwd=None, gain=None):
    n = u_lru.shape[0]
    n_chunks = n // LRU_CHUNK
    halo_per_chunk = LRU_CHUNK // HALO
    n_halo_blocks = n // HALO

    def cidx(s):
        return (n_chunks - 1 - s) if reverse else s

    def full(shape):
        return pl.BlockSpec(shape, lambda s: (0,) * len(shape))

    in_specs = [
        pl.BlockSpec((LRU_CHUNK, LRU_WIDTH), lambda s: (cidx(s), 0)),
        pl.BlockSpec((HALO, LRU_WIDTH),
                     lambda s: (jnp.maximum(cidx(s) * halo_per_chunk - 1, 0), 0)),
        pl.BlockSpec((HALO, LRU_WIDTH),
                     lambda s: (jnp.minimum((cidx(s) + 1) * halo_per_chunk, n_halo_blocks - 1), 0)),
        full((4, LRU_WIDTH)), full((1, LRU_WIDTH)),
        full((N_GATE_GROUPS, GATE_GROUP, 2 * GATE_GROUP)),
        full((N_GATE_GROUPS, 1, 2 * GATE_GROUP)),
        full((1, LRU_WIDTH)),
    ]
    args = [u_lru, u_lru, u_lru, conv_w, conv_b, wbd, gate_b, lam]
    if reverse:
        in_specs += [pl.BlockSpec((LRU_CHUNK, LRU_WIDTH), lambda s: (cidx(s), 0)),
                     pl.BlockSpec((LRU_CHUNK, LRU_WIDTH), lambda s: (cidx(s), 1)),
                     full((1, LRU_WIDTH))]
        args += [h_fwd, u_lru, gain]
    out_dtype = BF16 if reverse else F32
    return pl.pallas_call(
        functools.partial(_lru_kernel, reverse=reverse, seq_len=seq_len, n_chunks=n_chunks),
        grid=(n_chunks,),
        in_specs=in_specs,
        out_specs=pl.BlockSpec((LRU_CHUNK, LRU_WIDTH), lambda s: (cidx(s), 0)),
        out_shape=jax.ShapeDtypeStruct((n, LRU_WIDTH), out_dtype),
        scratch_shapes=[pltpu.VMEM((LRU_CHUNK + 2 * HALO, LRU_WIDTH), F32),
                        pltpu.VMEM((LRU_CHUNK, LRU_WIDTH), F32),
                        pltpu.VMEM((LRU_CHUNK, LRU_WIDTH), F32),
                        pltpu.VMEM((SUBLANES, LRU_WIDTH), F32)],
        compiler_params=_params("arbitrary"),
        name="lru_bwd" if reverse else "lru_fwd",
    )(*args)


NA_QROWS = 8
NA_KROWS = 16
NA_KBLK = 4
PAIR = 2 * NA_HEAD_DIM


def _natten_kernel(q_ref, k0, k1, k2, k3, v0, v1, v2, v3, bias_ref, gain_ref, o_ref,
                   ks_ref, vs_ref, s_ref, p_ref, *, rows):
    rblk = pl.program_id(1)
    for j, (kr, vr) in enumerate(((k0, v0), (k1, v1), (k2, v2), (k3, v3))):
        ks_ref[NA_KBLK * j:NA_KBLK * (j + 1)] = kr[0]
        vs_ref[NA_KBLK * j:NA_KBLK * (j + 1)] = vr[0]
    key_start = jnp.clip(rblk * NA_QROWS - NA_WIN_ROWS // 2, 0, rows - NA_KROWS)
    lane = lax.broadcasted_iota(jnp.int32, (GRID_W, PAIR), 1)
    lower = lane < NA_HEAD_DIM
    scale = NA_HEAD_DIM ** -0.5
    gain = gain_ref[...]

    def one_row(qr, _):
        r = rblk * NA_QROWS + qr
        win_start = jnp.clip(r - NA_WIN_ROWS // 2, 0, rows - NA_WIN_ROWS)
        d0 = win_start - key_start
        off = r - win_start
        q = q_ref[0, qr]
        pairs = range(NA_HEADS // 2)
        row_max, row_sum, outs = [], [], []
        for p in pairs:
            qp = q[:, p * PAIR:(p + 1) * PAIR] * scale
            q2 = jnp.concatenate([jnp.where(lower, qp, 0.0), jnp.where(lower, 0.0, qp)],
                                 axis=0).astype(BF16)
            kp = ks_ref[pl.ds(d0, NA_WIN_ROWS), :, p * PAIR:(p + 1) * PAIR]
            kp = kp.reshape(NA_WIN_ROWS * GRID_W, PAIR)
            s = lax.dot_general(q2, kp, (((1,), (1,)), ((), ())),
                                preferred_element_type=F32)
            s = s + bias_ref[off, p]
            s_ref[p] = s
            row_max.append(jnp.max(s, axis=-1, keepdims=True))
        for p in pairs:
            e = jnp.exp(s_ref[p] - row_max[p])
            row_sum.append(jnp.sum(e, axis=-1, keepdims=True))
            p_ref[p] = e.astype(BF16)
        for p in pairs:
            vp = vs_ref[pl.ds(d0, NA_WIN_ROWS), :, p * PAIR:(p + 1) * PAIR]
            vp = vp.reshape(NA_WIN_ROWS * GRID_W, PAIR)
            o = jnp.dot(p_ref[p], vp, preferred_element_type=F32) / row_sum[p]
            outs.append(jnp.where(lower, o[:GRID_W], o[GRID_W:]))
        y = jnp.concatenate(outs, axis=-1)
        o_ref[0, qr] = _rms_gain_rows(y, gain).astype(BF16)
        return 0

    lax.fori_loop(0, NA_QROWS, one_row, 0)


def _natten(u_att4, bias, gain):
    b, rows = u_att4.shape[0], u_att4.shape[1]
    n_kblk = rows // NA_KBLK

    def kv_spec(j, lane_block):
        def imap(bi, ri):
            start = jnp.clip(2 * ri - 1, 0, n_kblk - NA_KROWS // NA_KBLK)
            return (bi, start + j, 0, lane_block)
        return pl.BlockSpec((1, NA_KBLK, GRID_W, NA_WIDTH), imap)

    in_specs = ([pl.BlockSpec((1, NA_QROWS, GRID_W, NA_WIDTH), lambda bi, ri: (bi, ri, 0, 0))]
                + [kv_spec(j, 1) for j in range(4)] + [kv_spec(j, 2) for j in range(4)]
                + [pl.BlockSpec(bias.shape, lambda bi, ri: (0, 0, 0, 0)),
                   pl.BlockSpec((1, NA_WIDTH), lambda bi, ri: (0, 0))])
    return pl.pallas_call(
        functools.partial(_natten_kernel, rows=rows),
        grid=(b, rows // NA_QROWS),
        in_specs=in_specs,
        out_specs=pl.BlockSpec((1, NA_QROWS, GRID_W, NA_WIDTH), lambda bi, ri: (bi, ri, 0, 0)),
        out_shape=jax.ShapeDtypeStruct((b, rows, GRID_W, NA_WIDTH), BF16),
        scratch_shapes=[pltpu.VMEM((NA_KROWS, GRID_W, NA_WIDTH), BF16),
                        pltpu.VMEM((NA_KROWS, GRID_W, NA_WIDTH), BF16),
                        pltpu.VMEM((NA_HEADS // 2, 2 * GRID_W, NA_WIN_ROWS * GRID_W), F32),
                        pltpu.VMEM((NA_HEADS // 2, 2 * GRID_W, NA_WIN_ROWS * GRID_W), BF16)],
        compiler_params=_params("parallel", "parallel"),
        name="natten",
    )(*([u_att4] * 9), bias, gain)


def _natten_bias_table(rpb):
    off = np.arange(NA_WIN_ROWS)
    kr = np.arange(NA_WIN_ROWS)
    dr = kr[None, :] - off[:, None] + (NA_WIN_ROWS - 1)
    row_sel = (dr[:, :, None] == np.arange(2 * NA_WIN_ROWS - 1)).astype(np.float32)
    qc = np.arange(GRID_W)
    kc = np.arange(GRID_W)
    start = np.clip(qc - NA_WIN_COLS // 2, 0, GRID_W - NA_WIN_COLS)
    valid = (kc[None, :] >= start[:, None]) & (kc[None, :] < start[:, None] + NA_WIN_COLS)
    dc = kc[None, :] - qc[:, None] + (NA_WIN_COLS - 1)
    col_sel = ((dc[:, :, None] == np.arange(2 * NA_WIN_COLS - 1)) & valid[:, :, None])
    col_sel = col_sel.astype(np.float32)
    t = jnp.einsum('oka,hac,qjc->ohqkj', jnp.asarray(row_sel), rpb.astype(F32),
                   jnp.asarray(col_sel), precision=lax.Precision.HIGHEST)
    t = jnp.where(jnp.asarray(valid)[None, None, :, None, :], t, MASK_VALUE)
    return t.reshape(NA_WIN_ROWS, NA_HEADS // 2, 2 * GRID_W, NA_WIN_ROWS * GRID_W)


MEM_TILE = 512


def _mem_attn_kernel(q_ref, kv_ref, gain_ref, o_ref):
    q = q_ref[0]
    kv = kv_ref[0]
    scale = MEM_HEAD_DIM ** -0.5
    outs = []
    for h in range(MEM_HEADS):
        lo = h * MEM_HEAD_DIM
        s = lax.dot_general(q[:, lo:lo + MEM_HEAD_DIM], kv[:, lo:lo + MEM_HEAD_DIM],
                            (((1,), (1,)), ((), ())), preferred_element_type=F32) * scale
        m = jnp.max(s, axis=-1, keepdims=True)
        e = jnp.exp(s - m)
        l = jnp.sum(e, axis=-1, keepdims=True)
        v = kv[:, MEM_WIDTH + lo:MEM_WIDTH + lo + MEM_HEAD_DIM]
        outs.append(jnp.dot(e.astype(BF16), v, preferred_element_type=F32) / l)
    y = jnp.concatenate(outs, axis=-1)
    o_ref[0] = _rms_gain_rows(y, gain_ref[...]).astype(BF16)


def _mem_attn(u_att3, kv, gain):
    b, t = u_att3.shape[0], u_att3.shape[1]
    tm = min(MEM_TILE, t)
    return pl.pallas_call(
        _mem_attn_kernel,
        grid=(b, t // tm),
        in_specs=[pl.BlockSpec((1, tm, MEM_WIDTH), lambda bi, ti: (bi, ti, 3)),
                  pl.BlockSpec((1, N_MEM, 2 * MEM_WIDTH), lambda bi, ti: (bi, 0, 0)),
                  pl.BlockSpec((1, MEM_WIDTH), lambda bi, ti: (0, 0))],
        out_specs=pl.BlockSpec((1, tm, MEM_WIDTH), lambda bi, ti: (bi, ti, 0)),
        out_shape=jax.ShapeDtypeStruct((b, t, MEM_WIDTH), BF16),
        compiler_params=_params("parallel", "parallel"),
        name="mem_attn",
    )(u_att3, kv, gain)


OUT_TILE = 512


def _out_proj_kernel(yl_ref, yn_ref, ym_ref, w_ref, x_ref, g_ref, b_ref, o_ref):
    acc = jnp.dot(yl_ref[...], w_ref[0:LRU_WIDTH, :], preferred_element_type=F32)
    acc += jnp.dot(yn_ref[...], w_ref[LRU_WIDTH:LRU_WIDTH + NA_WIDTH, :],
                   preferred_element_type=F32)
    acc += jnp.dot(ym_ref[...], w_ref[LRU_WIDTH + NA_WIDTH:, :], preferred_element_type=F32)
    z = DN_ALPHA * x_ref[...] + acc
    o_ref[...] = _layer_norm_rows(z, g_ref[...], b_ref[...])


def _out_proj(y_lru, y_na, y_mem, w, x, g, b):
    n = x.shape[0]
    tm = min(OUT_TILE, n)
    row = lambda width: pl.BlockSpec((tm, width), lambda i: (i, 0))
    vec = pl.BlockSpec((1, D_MODEL), lambda i: (0, 0))
    return pl.pallas_call(
        _out_proj_kernel,
        grid=(n // tm,),
        in_specs=[row(LRU_WIDTH), row(NA_WIDTH), row(MEM_WIDTH),
                  pl.BlockSpec((D_MODEL, D_MODEL), lambda i: (0, 0)),
                  row(D_MODEL), vec, vec],
        out_specs=row(D_MODEL),
        out_shape=jax.ShapeDtypeStruct((n, D_MODEL), F32),
        compiler_params=_params("parallel"),
        name="out_proj_ln",
    )(y_lru, y_na, y_mem, w, x, g, b)


FFN_TILE = 512
FFN_CHUNK = 512


def _ffn_kernel(x_ref, w1_ref, w3_ref, w2_ref, g_ref, b_ref, o_ref, xb_ref, acc_ref):
    j = pl.program_id(1)

    @pl.when(j == 0)
    def _():
        xb_ref[...] = x_ref[...].astype(BF16)
        acc_ref[...] = jnp.zeros_like(acc_ref)

    xb = xb_ref[...]
    h1 = jnp.dot(xb, w1_ref[...], preferred_element_type=F32)
    h3 = jnp.dot(xb, w3_ref[...], preferred_element_type=F32)
    h = (h1 * jax.nn.sigmoid(h1) * h3).astype(BF16)
    acc_ref[...] += jnp.dot(h, w2_ref[...], preferred_element_type=F32)

    @pl.when(j == pl.num_programs(1) - 1)
    def _():
        z = DN_ALPHA * x_ref[...] + acc_ref[...]
        o_ref[...] = _layer_norm_rows(z, g_ref[...], b_ref[...])


def _ffn(x, w1, w3, w2, g, b):
    n = x.shape[0]
    tm = min(FFN_TILE, n)
    fc = FFN_CHUNK
    vec = pl.BlockSpec((1, D_MODEL), lambda i, j: (0, 0))
    return pl.pallas_call(
        _ffn_kernel,
        grid=(n // tm, D_FF // fc),
        in_specs=[pl.BlockSpec((tm, D_MODEL), lambda i, j: (i, 0)),
                  pl.BlockSpec((D_MODEL, fc), lambda i, j: (0, j)),
                  pl.BlockSpec((D_MODEL, fc), lambda i, j: (0, j)),
                  pl.BlockSpec((fc, D_MODEL), lambda i, j: (j, 0)),
                  vec, vec],
        out_specs=pl.BlockSpec((tm, D_MODEL), lambda i, j: (i, 0)),
        out_shape=jax.ShapeDtypeStruct((n, D_MODEL), F32),
        scratch_shapes=[pltpu.VMEM((tm, D_MODEL), BF16), pltpu.VMEM((tm, D_MODEL), F32)],
        compiler_params=_params("parallel", "arbitrary"),
        name="ffn_ln",
    )(x, w1, w3, w2, g, b)


MOE_TILE = 512
EXPERT_TILE = 512
GATE_LANES = 128
TOP_K = 2
INFO_P1, INFO_P2, INFO_R1, INFO_R2, INFO_E1, INFO_E2 = range(6)


def _router_kernel(x_ref, rw_ref, info_ref, count_ref, run_ref):
    step = pl.program_id(0)

    @pl.when(step == 0)
    def _():
        run_ref[...] = jnp.zeros_like(run_ref)

    logits = jnp.dot(x_ref[...], rw_ref[...], preferred_element_type=F32,
                     precision=lax.Precision.HIGHEST)
    tm = logits.shape[0]
    lane = lax.broadcasted_iota(jnp.int32, logits.shape, 1)
    logits = jnp.where(lane < N_EXPERTS, logits, -jnp.inf)
    m1 = jnp.max(logits, axis=-1, keepdims=True)
    i1 = jnp.min(jnp.where(logits == m1, lane, GATE_LANES), axis=-1, keepdims=True)
    rest = jnp.where(lane == i1, -jnp.inf, logits)
    m2 = jnp.max(rest, axis=-1, keepdims=True)
    i2 = jnp.min(jnp.where(rest == m2, lane, GATE_LANES), axis=-1, keepdims=True)
    e2 = jnp.exp(m2 - m1)
    p1 = 1.0 / (1.0 + e2)
    p2 = e2 / (1.0 + e2)

    sel = jnp.logical_or(lane == i1, lane == i2)
    onehot = jnp.where(sel, 1.0, 0.0).astype(BF16)
    r_i = lax.broadcasted_iota(jnp.int32, (tm, tm), 0)
    c_i = lax.broadcasted_iota(jnp.int32, (tm, tm), 1)
    lower = jnp.where(c_i < r_i, 1.0, 0.0).astype(BF16)
    rank = jnp.dot(lower, onehot, preferred_element_type=F32) + run_ref[0:1, :]
    run_ref[...] = run_ref[...] + jnp.sum(onehot.astype(F32), axis=0, keepdims=True)
    r1 = jnp.sum(jnp.where(lane == i1, rank, 0.0), axis=-1, keepdims=True)
    r2 = jnp.sum(jnp.where(lane == i2, rank, 0.0), axis=-1, keepdims=True)

    info = jnp.where(lane == INFO_P1, p1, 0.0)
    info = jnp.where(lane == INFO_P2, p2, info)
    info = jnp.where(lane == INFO_R1, r1, info)
    info = jnp.where(lane == INFO_R2, r2, info)
    info = jnp.where(lane == INFO_E1, i1.astype(F32), info)
    info = jnp.where(lane == INFO_E2, i2.astype(F32), info)
    info_ref[...] = info
    count_ref[...] = run_ref[...]


def _router(x, rw):
    n = x.shape[0]
    tm = min(MOE_TILE, n)
    return pl.pallas_call(
        _router_kernel,
        grid=(n // tm,),
        in_specs=[pl.BlockSpec((tm, D_MODEL), lambda i: (i, 0)),
                  pl.BlockSpec((D_MODEL, GATE_LANES), lambda i: (0, 0))],
        out_specs=[pl.BlockSpec((tm, GATE_LANES), lambda i: (i, 0)),
                   pl.BlockSpec((SUBLANES, GATE_LANES), lambda i: (0, 0))],
        out_shape=[jax.ShapeDtypeStruct((n, GATE_LANES), F32),
                   jax.ShapeDtypeStruct((SUBLANES, GATE_LANES), F32)],
        scratch_shapes=[pltpu.VMEM((SUBLANES, GATE_LANES), F32)],
        compiler_params=_params("arbitrary"),
        name="moe_router",
    )(x, rw)


def _row_copy(src, src_row, dst, dst_row, sem):
    return pltpu.make_async_copy(src.at[pl.ds(src_row, 1)], dst.at[pl.ds(dst_row, 1)], sem)


def _dispatch_kernel(fill_ref, x_ref, pos_hbm, xs_hbm, pos_smem, zero_ref, pos_sem, row_sem):
    step = pl.program_id(0)
    tm = x_ref.shape[0]

    @pl.when(step == 0)
    def _():
        zero_ref[...] = jnp.zeros_like(zero_ref)

        def clear_tile(start):
            start = pl.multiple_of(start, EXPERT_TILE)
            fill = pltpu.make_async_copy(zero_ref, xs_hbm.at[pl.ds(start, EXPERT_TILE)], row_sem)
            fill.start()
            fill.wait()

        for e in range(N_EXPERTS):
            @pl.when(fill_ref[e] >= 0)
            def _():
                clear_tile(fill_ref[e])

        def clear_idle(t, _):
            clear_tile(t * EXPERT_TILE)
            return 0

        lax.fori_loop(fill_ref[N_EXPERTS], xs_hbm.shape[0] // EXPERT_TILE, clear_idle, 0)

    base = pl.multiple_of(step * (TOP_K * tm), TOP_K * tm)
    pos_copy = pltpu.make_async_copy(pos_hbm.at[pl.ds(base, TOP_K * tm)], pos_smem, pos_sem)
    pos_copy.start()
    pos_copy.wait()

    def issue(r, _):
        for k in range(TOP_K):
            _row_copy(x_ref, r, xs_hbm, pos_smem[TOP_K * r + k], row_sem).start()
        return 0

    lax.fori_loop(0, tm, issue, 0)
    for k in range(TOP_K):
        pltpu.make_async_copy(x_ref, xs_hbm.at[pl.ds(0, tm)], row_sem).wait()


def _dispatch(x, pos, fill_start, n_rows):
    n = x.shape[0]
    tm = min(MOE_TILE, n)
    return pl.pallas_call(
        _dispatch_kernel,
        grid_spec=pltpu.PrefetchScalarGridSpec(
            num_scalar_prefetch=1,
            grid=(n // tm,),
            in_specs=[pl.BlockSpec((tm, D_MODEL), lambda i, f: (i, 0)),
                      pl.BlockSpec(memory_space=pl.ANY)],
            out_specs=pl.BlockSpec(memory_space=pl.ANY),
            scratch_shapes=[pltpu.SMEM((TOP_K * tm,), jnp.int32),
                            pltpu.VMEM((EXPERT_TILE, D_MODEL), F32),
                            pltpu.SemaphoreType.DMA(()), pltpu.SemaphoreType.DMA(())]),
        out_shape=jax.ShapeDtypeStruct((n_rows, D_MODEL), F32),
        compiler_params=_params("arbitrary"),
        name="moe_dispatch",
    )(fill_start, x, pos)


def _experts_kernel(te_ref, nv_ref, xs_ref, w1_ref, w3_ref, w2_ref, ys_ref):
    i = pl.program_id(0)

    @pl.when(i < nv_ref[0])
    def _():
        xb = xs_ref[...].astype(BF16)
        h1 = jnp.dot(xb, w1_ref[0], preferred_element_type=F32)
        h3 = jnp.dot(xb, w3_ref[0], preferred_element_type=F32)
        h = (h1 * jax.nn.sigmoid(h1) * h3).astype(BF16)
        ys_ref[...] = jnp.dot(h, w2_ref[0], preferred_element_type=F32)

    @pl.when(i >= nv_ref[0])
    def _():
        ys_ref[...] = jnp.zeros_like(ys_ref)


def _experts(xs, tile_expert, n_valid, w1, w3, w2):
    n_rows = xs.shape[0]
    te = EXPERT_TILE
    wspec = lambda shape: pl.BlockSpec((1,) + shape, lambda i, t, nv: (t[i], 0, 0))
    return pl.pallas_call(
        _experts_kernel,
        grid_spec=pltpu.PrefetchScalarGridSpec(
            num_scalar_prefetch=2,
            grid=(n_rows // te,),
            in_specs=[pl.BlockSpec((te, D_MODEL), lambda i, t, nv: (jnp.minimum(i, nv[0] - 1), 0)),
                      wspec((D_MODEL, D_FF_EXPERT)), wspec((D_MODEL, D_FF_EXPERT)),
                      wspec((D_FF_EXPERT, D_MODEL))],
            out_specs=pl.BlockSpec((te, D_MODEL), lambda i, t, nv: (i, 0))),
        out_shape=jax.ShapeDtypeStruct((n_rows, D_MODEL), F32),
        compiler_params=_params("arbitrary"),
        name="moe_experts",
    )(tile_expert, n_valid, xs, w1, w3, w2)


def _combine_kernel(x_ref, info_ref, g_ref, b_ref, pos_hbm, ys_hbm, o_ref,
                    pos_smem, y1_ref, y2_ref, pos_sem, row_sem):
    step = pl.program_id(0)
    tm = x_ref.shape[0]
    base = pl.multiple_of(step * (TOP_K * tm), TOP_K * tm)
    pos_copy = pltpu.make_async_copy(pos_hbm.at[pl.ds(base, TOP_K * tm)], pos_smem, pos_sem)
    pos_copy.start()
    pos_copy.wait()

    def issue(r, _):
        _row_copy(ys_hbm, pos_smem[TOP_K * r], y1_ref, r, row_sem).start()
        _row_copy(ys_hbm, pos_smem[TOP_K * r + 1], y2_ref, r, row_sem).start()
        return 0

    lax.fori_loop(0, tm, issue, 0)
    for y_ref in (y1_ref, y2_ref):
        pltpu.make_async_copy(ys_hbm.at[pl.ds(0, tm)], y_ref, row_sem).wait()

    info = info_ref[...]
    p1 = info[:, INFO_P1:INFO_P1 + 1]
    p2 = info[:, INFO_P2:INFO_P2 + 1]
    z = DN_ALPHA * x_ref[...] + (p1 * y1_ref[...] + p2 * y2_ref[...])
    o_ref[...] = _layer_norm_rows(z, g_ref[...], b_ref[...])


def _combine(x, info, pos, ys, g, b):
    n = x.shape[0]
    tm = min(MOE_TILE, n)
    vec = pl.BlockSpec((1, D_MODEL), lambda i: (0, 0))
    return pl.pallas_call(
        _combine_kernel,
        grid=(n // tm,),
        in_specs=[pl.BlockSpec((tm, D_MODEL), lambda i: (i, 0)),
                  pl.BlockSpec((tm, GATE_LANES), lambda i: (i, 0)),
                  vec, vec,
                  pl.BlockSpec(memory_space=pl.ANY), pl.BlockSpec(memory_space=pl.ANY)],
        out_specs=pl.BlockSpec((tm, D_MODEL), lambda i: (i, 0)),
        out_shape=jax.ShapeDtypeStruct((n, D_MODEL), F32),
        scratch_shapes=[pltpu.SMEM((TOP_K * tm,), jnp.int32),
                        pltpu.VMEM((tm, D_MODEL), F32), pltpu.VMEM((tm, D_MODEL), F32),
                        pltpu.SemaphoreType.DMA(()), pltpu.SemaphoreType.DMA(())],
        compiler_params=_params("arbitrary"),
        name="moe_combine_ln",
    )(x, info, g, b, pos, ys)


def _moe(x, rw, w1, w3, w2, g, b):
    n = x.shape[0]
    te = EXPERT_TILE
    info, counts = _router(x, rw)
    cnt = counts[0, :N_EXPERTS].astype(jnp.int32)
    padded = ((cnt + te - 1) // te) * te
    ends = jnp.cumsum(padded)
    offsets = ends - padded
    experts = jnp.arange(N_EXPERTS, dtype=jnp.int32)

    def position(e_lane, r_lane):
        e = info[:, e_lane].astype(jnp.int32)
        r = info[:, r_lane].astype(jnp.int32)
        return jnp.sum(jnp.where(e[:, None] == experts[None, :], offsets[None, :], 0), axis=1) + r

    pos = jnp.stack([position(INFO_E1, INFO_R1), position(INFO_E2, INFO_R2)], axis=1)
    pos = pos.reshape(TOP_K * n)
    max_tiles = TOP_K * n // te + N_EXPERTS
    n_valid = (ends[-1] // te).astype(jnp.int32)
    tile_start = jnp.arange(max_tiles, dtype=jnp.int32) * te
    tile_start = jnp.minimum(tile_start, (n_valid - 1) * te)
    tile_expert = jnp.sum((tile_start[:, None] >= ends[None, :]).astype(jnp.int32), axis=1)
    fill_start = jnp.where(padded > 0, ends - te, -1).astype(jnp.int32)
    fill_start = jnp.concatenate([fill_start, n_valid.reshape(1)])

    xs = _dispatch(x, pos, fill_start, max_tiles * te)
    ys = _experts(xs, tile_expert, n_valid.reshape(1), w1, w3, w2)
    return _combine(x, info, pos, ys, g, b)


def _block_diag_gates(gate_w, gate_b):
    per = GATE_GROUP // LRU_BLOCK
    w = gate_w.reshape(2, N_GATE_GROUPS, per, LRU_BLOCK, LRU_BLOCK)
    eye = jnp.eye(per, dtype=gate_w.dtype)
    wbd = w[:, :, :, :, None, :] * eye[None, None, :, None, :, None]
    wbd = jnp.transpose(wbd, (1, 2, 3, 0, 4, 5)).reshape(N_GATE_GROUPS, GATE_GROUP, 2 * GATE_GROUP)
    b = gate_b.reshape(2, N_GATE_GROUPS, GATE_GROUP)
    b = jnp.transpose(b, (1, 0, 2)).reshape(N_GATE_GROUPS, 1, 2 * GATE_GROUP)
    return wbd.astype(BF16), b.astype(F32)


def _prepare_layer(l, w_in, conv_w, conv_b, lru_gate_w, lru_gate_b, lru_lambda, rpb, w_mem_kv,
                   mix_gain, w_out, ln1_g, ln1_b, ffn_w1, ffn_w3, ffn_w2, router_w, moe_w1,
                   moe_w3, moe_w2, ln2_g, ln2_b):
    p = {}
    p['w_in'] = w_in[l].astype(BF16)
    p['conv_w'] = conv_w[l].astype(F32)
    p['conv_b'] = conv_b[l].astype(F32).reshape(1, LRU_WIDTH)
    p['gates'] = [_block_diag_gates(lru_gate_w[l, d], lru_gate_b[l, d]) for d in range(2)]
    p['lam'] = [lru_lambda[l, d].astype(F32).reshape(1, LRU_WIDTH) for d in range(2)]
    p['na_bias'] = _natten_bias_table(rpb[l])
    p['w_mem_kv'] = w_mem_kv[l].astype(BF16)
    gain = mix_gain[l].astype(F32)
    p['gain_lru'] = gain[:LRU_WIDTH].reshape(1, LRU_WIDTH)
    p['gain_na'] = gain[LRU_WIDTH:LRU_WIDTH + NA_WIDTH].reshape(1, NA_WIDTH)
    p['gain_mem'] = gain[LRU_WIDTH + NA_WIDTH:].reshape(1, MEM_WIDTH)
    p['w_out'] = w_out[l].astype(BF16)
    vec = lambda v: v.astype(F32).reshape(1, D_MODEL)
    p['ln1'] = (vec(ln1_g[l]), vec(ln1_b[l]))
    p['ln2'] = (vec(ln2_g[l]), vec(ln2_b[l]))
    j = l // 2
    if l % 2 == 0:
        p['ffn'] = (ffn_w1[j].astype(BF16), ffn_w3[j].astype(BF16), ffn_w2[j].astype(BF16))
    else:
        rw = jnp.zeros((D_MODEL, GATE_LANES), F32).at[:, :N_EXPERTS].set(router_w[j].astype(F32))
        p['moe'] = (rw, moe_w1[j].astype(BF16), moe_w3[j].astype(BF16), moe_w2[j].astype(BF16))
    return p


def _trunk(x, mem, layers):
    b, t, _ = x.shape
    n = b * t
    rows = t // GRID_W
    x = x.reshape(n, D_MODEL)
    mem2 = mem.reshape(b * N_MEM, D_MODEL)
    for p in layers:
        u_lru, u_att = _in_proj(x, p['w_in'])
        (wbd_f, gb_f), (wbd_b, gb_b) = p['gates']
        h_fwd = _lru_direction(u_lru, p['conv_w'], p['conv_b'], wbd_f, gb_f, p['lam'][0], t, False)
        y_lru = _lru_direction(u_lru, p['conv_w'], p['conv_b'], wbd_b, gb_b, p['lam'][1], t, True,
                               h_fwd=h_fwd, gain=p['gain_lru'])
        y_na = _natten(u_att.reshape(b, rows, GRID_W, 2048), p['na_bias'], p['gain_na'])
        kv = _matmul(mem2, p['w_mem_kv'], BF16).reshape(b, N_MEM, 2 * MEM_WIDTH)
        y_mem = _mem_attn(u_att.reshape(b, t, 2048), kv, p['gain_mem'])
        x = _out_proj(y_lru, y_na.reshape(n, NA_WIDTH), y_mem.reshape(n, MEM_WIDTH),
                      p['w_out'], x, *p['ln1'])
        if 'ffn' in p:
            x = _ffn(x, *p['ffn'], *p['ln2'])
        else:
            x = _moe(x, *p['moe'], *p['ln2'])
    return x.reshape(b, t, D_MODEL)


def kernel(x_prompt, x_sample, mem_prompt, mem_sample, w_in, conv_w, conv_b, lru_gate_w,
           lru_gate_b, lru_lambda, rpb, w_mem_kv, mix_gain, w_out, ln1_g, ln1_b, ffn_w1,
           ffn_w3, ffn_w2, router_w, moe_w1, moe_w3, moe_w2, ln2_g, ln2_b):
    weights = (w_in, conv_w, conv_b, lru_gate_w, lru_gate_b, lru_lambda, rpb, w_mem_kv,
               mix_gain, w_out, ln1_g, ln1_b, ffn_w1, ffn_w3, ffn_w2, router_w, moe_w1,
               moe_w3, moe_w2, ln2_g, ln2_b)
    layers = [_prepare_layer(l, *weights) for l in range(DEPTH)]
    return (_trunk(x_prompt, mem_prompt, layers), _trunk(x_sample, mem_sample, layers))
```

```python
import functools

import numpy as np
import jax
import jax.numpy as jnp
from jax import lax
from jax.experimental import pallas as pl
from jax.experimental.pallas import tpu as pltpu

F32 = jnp.float32
BF16 = jnp.bfloat16

D_MODEL = 2048
DEPTH = 4
GRID_W = 64
LRU_WIDTH = 1024
LRU_BLOCKS = 16
LRU_BLOCK = 64
LRU_C = 8.0
NA_WIDTH = 512
NA_HEADS = 8
NA_HEAD_DIM = 64
NA_WIN_ROWS = 8
NA_WIN_COLS = 16
MASK_VALUE = -1e30
MEM_WIDTH = 512
MEM_HEADS = 4
MEM_HEAD_DIM = 128
N_MEM = 256
IN_WIDTH = 4096
D_FF = 4096
N_EXPERTS = 8
D_FF_EXPERT = 1024
DN_ALPHA = (2 * DEPTH) ** 0.25
LN_EPS = 1e-5
RMS_EPS = 1e-6

VMEM_LIMIT_BYTES = 56 * 1024 * 1024
GATE_GROUP = 256
N_GATE_GROUPS = LRU_WIDTH // GATE_GROUP
SUBLANES = 8
ROW_DMA_UNROLL = 4


def _params(*semantics):
    return pltpu.CompilerParams(dimension_semantics=semantics,
                                vmem_limit_bytes=VMEM_LIMIT_BYTES)


def _layer_norm_rows(z, g, b):
    mu = jnp.mean(z, axis=-1, keepdims=True)
    zc = z - mu
    var = jnp.mean(zc * zc, axis=-1, keepdims=True)
    return zc * lax.rsqrt(var + LN_EPS) * g + b


def _rms_gain_rows(y, gain):
    ms = jnp.mean(y * y, axis=-1, keepdims=True)
    return y * lax.rsqrt(ms + RMS_EPS) * gain


def _in_proj_kernel(x_ref, w_ref, lru_ref, att_ref, xb_ref):
    j = pl.program_id(1)

    @pl.when(j == 0)
    def _():
        xb_ref[...] = x_ref[...].astype(BF16)

    y = jnp.dot(xb_ref[...], w_ref[...], preferred_element_type=F32)

    @pl.when(j < 2)
    def _():
        lru_ref[...] = y

    @pl.when(j >= 2)
    def _():
        att_ref[...] = y.astype(BF16)


def _in_proj(x, w):
    n = x.shape[0]
    tm = min(1024, n)
    tn = 1024
    return pl.pallas_call(
        _in_proj_kernel,
        grid=(n // tm, IN_WIDTH // tn),
        in_specs=[pl.BlockSpec((tm, D_MODEL), lambda i, j: (i, 0)),
                  pl.BlockSpec((D_MODEL, tn), lambda i, j: (0, j))],
        out_specs=[pl.BlockSpec((tm, tn), lambda i, j: (i, jnp.minimum(j, 1))),
                   pl.BlockSpec((tm, tn), lambda i, j: (i, jnp.maximum(j - 2, 0)))],
        out_shape=[jax.ShapeDtypeStruct((n, 2 * LRU_WIDTH), F32),
                   jax.ShapeDtypeStruct((n, 2048), BF16)],
        scratch_shapes=[pltpu.VMEM((tm, D_MODEL), BF16)],
        compiler_params=_params("parallel", "arbitrary"),
        name="in_proj",
    )(x, w)


def _matmul_kernel(x_ref, w_ref, o_ref):
    o_ref[...] = jnp.dot(x_ref[...].astype(BF16), w_ref[...],
                         preferred_element_type=F32).astype(o_ref.dtype)


def _matmul(x, w, out_dtype):
    m, k = x.shape
    n = w.shape[1]
    tm = min(512, m)
    return pl.pallas_call(
        _matmul_kernel,
        grid=(m // tm,),
        in_specs=[pl.BlockSpec((tm, k), lambda i: (i, 0)),
                  pl.BlockSpec((k, n), lambda i: (0, 0))],
        out_specs=pl.BlockSpec((tm, n), lambda i: (i, 0)),
        out_shape=jax.ShapeDtypeStruct((m, n), out_dtype),
        compiler_params=_params("parallel"),
        name="mem_kv_proj",
    )(x, w)


LRU_CHUNK = 512
LRU_SUB = 128
HALO = SUBLANES


def _gelu_tanh(x):
    return 0.5 * x * (1.0 + jnp.tanh(0.7978845608028654 * (x + 0.044715 * (x * x * x))))


def _lru_kernel(*refs, reverse, seq_len, n_chunks):
    if reverse:
        (x_ref, prev_ref, next_ref, cw_ref, cb_ref, wbd_ref, gb_ref, lam_ref,
         hf_ref, gl_ref, gain_ref, o_ref, a_ref, u_ref, carry_ref) = refs
        h_ref = u_ref
    else:
        (x_ref, prev_ref, next_ref, cw_ref, cb_ref, wbd_ref, gb_ref, lam_ref,
         o_ref, a_ref, u_ref, carry_ref) = refs
        h_ref = o_ref

    step = pl.program_id(0)
    chunk = (n_chunks - 1 - step) if reverse else step
    pos = (chunk * LRU_CHUNK) % seq_len
    seq_first = pos == 0
    seq_last = pos == seq_len - LRU_CHUNK

    halo_before = jnp.where(seq_first, 0.0, prev_ref[...])
    halo_after = jnp.where(seq_last, 0.0, next_ref[...])

    @pl.when(seq_last if reverse else seq_first)
    def _():
        carry_ref[...] = jnp.zeros_like(carry_ref)

    lam = lam_ref[...]
    neg = -lam
    softplus = jnp.maximum(neg, 0.0) + jnp.log(1.0 + jnp.exp(-jnp.abs(neg)))
    c_exp = (-0.5 * LRU_C * 1.4426950408889634) * softplus
    cw = cw_ref[...]
    cb = cb_ref[...]
    win = LRU_SUB + 2 * HALO

    def sub_block(s, _):
        r0 = pl.multiple_of(s * LRU_SUB, LRU_SUB)
        above = x_ref[pl.ds(pl.multiple_of(jnp.maximum(r0 - HALO, 0), HALO), HALO), :]
        below = x_ref[pl.ds(pl.multiple_of(jnp.minimum(r0 + LRU_SUB, LRU_CHUNK - HALO), HALO),
                            HALO), :]
        xw = jnp.concatenate([jnp.where(s == 0, halo_before, above),
                              x_ref[pl.ds(r0, LRU_SUB), :],
                              jnp.where(s == LRU_CHUNK // LRU_SUB - 1, halo_after, below)],
                             axis=0)
        xc = (cw[0:1] * pltpu.roll(xw, 2, 0) + cw[1:2] * pltpu.roll(xw, 1, 0)
              + cw[2:3] * xw + cw[3:4] * pltpu.roll(xw, win - 1, 0))
        xc = xc[HALO:HALO + LRU_SUB] + cb
        xcb = xc.astype(BF16)
        half_xc = 0.5 * xc
        for g in range(N_GATE_GROUPS):
            lo = g * GATE_GROUP
            half = jnp.dot(xcb[:, lo:lo + GATE_GROUP], wbd_ref[g],
                           preferred_element_type=F32) + gb_ref[g]
            t_r = jnp.tanh(half[:, :GATE_GROUP])
            t_i = jnp.tanh(half[:, GATE_GROUP:])
            cg = c_exp[:, lo:lo + GATE_GROUP]
            a = jnp.exp2(cg * t_r + cg)
            y = 1.0 - a * a
            mult = jnp.where(y > 0.0, y * lax.rsqrt(y), 0.0)
            a_ref[pl.ds(r0, LRU_SUB), lo:lo + GATE_GROUP] = a
            u_ref[pl.ds(r0, LRU_SUB), lo:lo + GATE_GROUP] = (
                (mult * half_xc[:, lo:lo + GATE_GROUP]) * (t_i + 1.0))
        return 0

    lax.fori_loop(0, LRU_CHUNK // LRU_SUB, sub_block, 0)

    row = lax.broadcasted_iota(jnp.int32, (SUBLANES, LRU_WIDTH), 0)
    n_blocks = LRU_CHUNK // SUBLANES

    def scan_block(k, carry):
        blk = (n_blocks - 1 - k) if reverse else k
        r0 = pl.multiple_of(blk * SUBLANES, SUBLANES)
        a = a_ref[pl.ds(r0, SUBLANES), :]
        u = u_ref[pl.ds(r0, SUBLANES), :]
        for d in (1, 2, 4):
            if reverse:
                keep = row < SUBLANES - d
                shift = SUBLANES - d
            else:
                keep = row >= d
                shift = d
            u = u + a * jnp.where(keep, pltpu.roll(u, shift, 0), 0.0)
            a = a * jnp.where(keep, pltpu.roll(a, shift, 0), 1.0)
        h = u + a * carry
        h_ref[pl.ds(r0, SUBLANES), :] = h
        edge = h[0:1, :] if reverse else h[SUBLANES - 1:SUBLANES, :]
        return jnp.broadcast_to(edge, (SUBLANES, LRU_WIDTH))

    carry_ref[...] = lax.fori_loop(0, n_blocks, scan_block, carry_ref[...], unroll=2)

    if reverse:
        gain = gain_ref[...]

        def out_block(s, _):
            r0 = pl.multiple_of(s * LRU_SUB, LRU_SUB)
            h = u_ref[pl.ds(r0, LRU_SUB), :] + hf_ref[pl.ds(r0, LRU_SUB), :]
            y = h * _gelu_tanh(gl_ref[pl.ds(r0, LRU_SUB), :])
            o_ref[pl.ds(r0, LRU_SUB), :] = _rms_gain_rows(y, gain).astype(BF16)
            return 0

        lax.fori_loop(0, LRU_CHUNK // LRU_SUB, out_block, 0)


def _lru_direction(u_lru, conv_w, conv_b, wbd, gate_b, lam, seq_len, reverse,
                   h_fwd=None, gain=None):
    n = u_lru.shape[0]
    n_chunks = n // LRU_CHUNK
    halo_per_chunk = LRU_CHUNK // HALO
    n_halo_blocks = n // HALO

    def cidx(s):
        return (n_chunks - 1 - s) if reverse else s

    def full(shape):
        return pl.BlockSpec(shape, lambda s: (0,) * len(shape))

    in_specs = [
        pl.BlockSpec((LRU_CHUNK, LRU_WIDTH), lambda s: (cidx(s), 0)),
        pl.BlockSpec((HALO, LRU_WIDTH),
                     lambda s: (jnp.maximum(cidx(s) * halo_per_chunk - 1, 0), 0)),
        pl.BlockSpec((HALO, LRU_WIDTH),
                     lambda s: (jnp.minimum((cidx(s) + 1) * halo_per_chunk, n_halo_blocks - 1), 0)),
        full((4, LRU_WIDTH)), full((1, LRU_WIDTH)),
        full((N_GATE_GROUPS, GATE_GROUP, 2 * GATE_GROUP)),
        full((N_GATE_GROUPS, 1, 2 * GATE_GROUP)),
        full((1, LRU_WIDTH)),
    ]
    args = [u_lru, u_lru, u_lru, conv_w, conv_b, wbd, gate_b, lam]
    if reverse:
        in_specs += [pl.BlockSpec((LRU_CHUNK, LRU_WIDTH), lambda s: (cidx(s), 0)),
                     pl.BlockSpec((LRU_CHUNK, LRU_WIDTH), lambda s: (cidx(s), 1)),
                     full((1, LRU_WIDTH))]
        args += [h_fwd, u_lru, gain]
    out_dtype = BF16 if reverse else F32
    return pl.pallas_call(
        functools.partial(_lru_kernel, reverse=reverse, seq_len=seq_len, n_chunks=n_chunks),
        grid=(n_chunks,),
        in_specs=in_specs,
        out_specs=pl.BlockSpec((LRU_CHUNK, LRU_WIDTH), lambda s: (cidx(s), 0)),
        out_shape=jax.ShapeDtypeStruct((n, LRU_WIDTH), out_dtype),
        scratch_shapes=[pltpu.VMEM((LRU_CHUNK, LRU_WIDTH), F32),
                        pltpu.VMEM((LRU_CHUNK, LRU_WIDTH), F32),
                        pltpu.VMEM((SUBLANES, LRU_WIDTH), F32)],
        compiler_params=_params("arbitrary"),
        name="lru_bwd" if reverse else "lru_fwd",
    )(*args)


NA_QROWS = 8
NA_KROWS = 16
NA_KBLK = 4
PAIR = 2 * NA_HEAD_DIM


def _natten_kernel(q_ref, k0, k1, k2, k3, v0, v1, v2, v3, bias_ref, gain_ref, o_ref,
                   ks_ref, vs_ref, s_ref, p_ref, *, rows):
    rblk = pl.program_id(1)
    for j, (kr, vr) in enumerate(((k0, v0), (k1, v1), (k2, v2), (k3, v3))):
        ks_ref[NA_KBLK * j:NA_KBLK * (j + 1)] = kr[0]
        vs_ref[NA_KBLK * j:NA_KBLK * (j + 1)] = vr[0]
    key_start = jnp.clip(rblk * NA_QROWS - NA_WIN_ROWS // 2, 0, rows - NA_KROWS)
    lane = lax.broadcasted_iota(jnp.int32, (GRID_W, PAIR), 1)
    lower = lane < NA_HEAD_DIM
    scale = NA_HEAD_DIM ** -0.5
    gain = gain_ref[...]

    def one_row(qr, _):
        r = rblk * NA_QROWS + qr
        win_start = jnp.clip(r - NA_WIN_ROWS // 2, 0, rows - NA_WIN_ROWS)
        d0 = win_start - key_start
        off = r - win_start
        q = q_ref[0, qr]
        pairs = range(NA_HEADS // 2)
        row_max, row_sum, outs = [], [], []
        for p in pairs:
            qp = q[:, p * PAIR:(p + 1) * PAIR] * scale
            q2 = jnp.concatenate([jnp.where(lower, qp, 0.0), jnp.where(lower, 0.0, qp)],
                                 axis=0).astype(BF16)
            kp = ks_ref[pl.ds(d0, NA_WIN_ROWS), :, p * PAIR:(p + 1) * PAIR]
            kp = kp.reshape(NA_WIN_ROWS * GRID_W, PAIR)
            s = lax.dot_general(q2, kp, (((1,), (1,)), ((), ())),
                                preferred_element_type=F32)
            s = s + bias_ref[off, p]
            s_ref[p] = s
            row_max.append(jnp.max(s, axis=-1, keepdims=True))
        for p in pairs:
            e = jnp.exp(s_ref[p] - row_max[p])
            row_sum.append(jnp.sum(e, axis=-1, keepdims=True))
            p_ref[p] = e.astype(BF16)
        for p in pairs:
            vp = vs_ref[pl.ds(d0, NA_WIN_ROWS), :, p * PAIR:(p + 1) * PAIR]
            vp = vp.reshape(NA_WIN_ROWS * GRID_W, PAIR)
            o = jnp.dot(p_ref[p], vp, preferred_element_type=F32) / row_sum[p]
            outs.append(jnp.where(lower, o[:GRID_W], o[GRID_W:]))
        y = jnp.concatenate(outs, axis=-1)
        o_ref[0, qr] = _rms_gain_rows(y, gain).astype(BF16)
        return 0

    lax.fori_loop(0, NA_QROWS, one_row, 0)


def _natten(u_att4, bias, gain):
    b, rows = u_att4.shape[0], u_att4.shape[1]
    n_kblk = rows // NA_KBLK

    def kv_spec(j, lane_block):
        def imap(bi, ri):
            start = jnp.clip(2 * ri - 1, 0, n_kblk - NA_KROWS // NA_KBLK)
            return (bi, start + j, 0, lane_block)
        return pl.BlockSpec((1, NA_KBLK, GRID_W, NA_WIDTH), imap)

    in_specs = ([pl.BlockSpec((1, NA_QROWS, GRID_W, NA_WIDTH), lambda bi, ri: (bi, ri, 0, 0))]
                + [kv_spec(j, 1) for j in range(4)] + [kv_spec(j, 2) for j in range(4)]
                + [pl.BlockSpec(bias.shape, lambda bi, ri: (0, 0, 0, 0)),
                   pl.BlockSpec((1, NA_WIDTH), lambda bi, ri: (0, 0))])
    return pl.pallas_call(
        functools.partial(_natten_kernel, rows=rows),
        grid=(b, rows // NA_QROWS),
        in_specs=in_specs,
        out_specs=pl.BlockSpec((1, NA_QROWS, GRID_W, NA_WIDTH), lambda bi, ri: (bi, ri, 0, 0)),
        out_shape=jax.ShapeDtypeStruct((b, rows, GRID_W, NA_WIDTH), BF16),
        scratch_shapes=[pltpu.VMEM((NA_KROWS, GRID_W, NA_WIDTH), BF16),
                        pltpu.VMEM((NA_KROWS, GRID_W, NA_WIDTH), BF16),
                        pltpu.VMEM((NA_HEADS // 2, 2 * GRID_W, NA_WIN_ROWS * GRID_W), F32),
                        pltpu.VMEM((NA_HEADS // 2, 2 * GRID_W, NA_WIN_ROWS * GRID_W), BF16)],
        compiler_params=_params("parallel", "parallel"),
        name="natten",
    )(*([u_att4] * 9), bias, gain)


def _natten_bias_table(rpb):
    off = np.arange(NA_WIN_ROWS)
    kr = np.arange(NA_WIN_ROWS)
    dr = kr[None, :] - off[:, None] + (NA_WIN_ROWS - 1)
    row_sel = (dr[:, :, None] == np.arange(2 * NA_WIN_ROWS - 1)).astype(np.float32)
    qc = np.arange(GRID_W)
    kc = np.arange(GRID_W)
    start = np.clip(qc - NA_WIN_COLS // 2, 0, GRID_W - NA_WIN_COLS)
    valid = (kc[None, :] >= start[:, None]) & (kc[None, :] < start[:, None] + NA_WIN_COLS)
    dc = kc[None, :] - qc[:, None] + (NA_WIN_COLS - 1)
    col_sel = ((dc[:, :, None] == np.arange(2 * NA_WIN_COLS - 1)) & valid[:, :, None])
    col_sel = col_sel.astype(np.float32)
    t = jnp.einsum('oka,hac,qjc->ohqkj', jnp.asarray(row_sel), rpb.astype(F32),
                   jnp.asarray(col_sel), precision=lax.Precision.HIGHEST)
    t = jnp.where(jnp.asarray(valid)[None, None, :, None, :], t, MASK_VALUE)
    return t.reshape(NA_WIN_ROWS, NA_HEADS // 2, 2 * GRID_W, NA_WIN_ROWS * GRID_W)


MEM_TILE = 512


def _mem_attn_kernel(q_ref, kv_ref, gain_ref, o_ref):
    q = q_ref[0]
    kv = kv_ref[0]
    scale = MEM_HEAD_DIM ** -0.5
    outs = []
    for h in range(MEM_HEADS):
        lo = h * MEM_HEAD_DIM
        s = lax.dot_general(q[:, lo:lo + MEM_HEAD_DIM], kv[:, lo:lo + MEM_HEAD_DIM],
                            (((1,), (1,)), ((), ())), preferred_element_type=F32) * scale
        m = jnp.max(s, axis=-1, keepdims=True)
        e = jnp.exp(s - m)
        l = jnp.sum(e, axis=-1, keepdims=True)
        v = kv[:, MEM_WIDTH + lo:MEM_WIDTH + lo + MEM_HEAD_DIM]
        outs.append(jnp.dot(e.astype(BF16), v, preferred_element_type=F32) / l)
    y = jnp.concatenate(outs, axis=-1)
    o_ref[0] = _rms_gain_rows(y, gain_ref[...]).astype(BF16)


def _mem_attn(u_att3, kv, gain):
    b, t = u_att3.shape[0], u_att3.shape[1]
    tm = min(MEM_TILE, t)
    return pl.pallas_call(
        _mem_attn_kernel,
        grid=(b, t // tm),
        in_specs=[pl.BlockSpec((1, tm, MEM_WIDTH), lambda bi, ti: (bi, ti, 3)),
                  pl.BlockSpec((1, N_MEM, 2 * MEM_WIDTH), lambda bi, ti: (bi, 0, 0)),
                  pl.BlockSpec((1, MEM_WIDTH), lambda bi, ti: (0, 0))],
        out_specs=pl.BlockSpec((1, tm, MEM_WIDTH), lambda bi, ti: (bi, ti, 0)),
        out_shape=jax.ShapeDtypeStruct((b, t, MEM_WIDTH), BF16),
        compiler_params=_params("parallel", "parallel"),
        name="mem_attn",
    )(u_att3, kv, gain)


OUT_TILE = 512


def _out_proj_kernel(yl_ref, yn_ref, ym_ref, w_ref, x_ref, g_ref, b_ref, o_ref):
    acc = jnp.dot(yl_ref[...], w_ref[0:LRU_WIDTH, :], preferred_element_type=F32)
    acc += jnp.dot(yn_ref[...], w_ref[LRU_WIDTH:LRU_WIDTH + NA_WIDTH, :],
                   preferred_element_type=F32)
    acc += jnp.dot(ym_ref[...], w_ref[LRU_WIDTH + NA_WIDTH:, :], preferred_element_type=F32)
    z = DN_ALPHA * x_ref[...] + acc
    o_ref[...] = _layer_norm_rows(z, g_ref[...], b_ref[...])


def _out_proj(y_lru, y_na, y_mem, w, x, g, b):
    n = x.shape[0]
    tm = min(OUT_TILE, n)
    row = lambda width: pl.BlockSpec((tm, width), lambda i: (i, 0))
    vec = pl.BlockSpec((1, D_MODEL), lambda i: (0, 0))
    return pl.pallas_call(
        _out_proj_kernel,
        grid=(n // tm,),
        in_specs=[row(LRU_WIDTH), row(NA_WIDTH), row(MEM_WIDTH),
                  pl.BlockSpec((D_MODEL, D_MODEL), lambda i: (0, 0)),
                  row(D_MODEL), vec, vec],
        out_specs=row(D_MODEL),
        out_shape=jax.ShapeDtypeStruct((n, D_MODEL), F32),
        compiler_params=_params("parallel"),
        name="out_proj_ln",
    )(y_lru, y_na, y_mem, w, x, g, b)


FFN_TILE = 512
FFN_CHUNK = 512


def _ffn_kernel(x_ref, w1_ref, w3_ref, w2_ref, g_ref, b_ref, o_ref, xb_ref, acc_ref):
    j = pl.program_id(1)

    @pl.when(j == 0)
    def _():
        xb_ref[...] = x_ref[...].astype(BF16)
        acc_ref[...] = jnp.zeros_like(acc_ref)

    xb = xb_ref[...]
    h1 = jnp.dot(xb, w1_ref[...], preferred_element_type=F32)
    h3 = jnp.dot(xb, w3_ref[...], preferred_element_type=F32)
    h = (h1 * jax.nn.sigmoid(h1) * h3).astype(BF16)
    acc_ref[...] += jnp.dot(h, w2_ref[...], preferred_element_type=F32)

    @pl.when(j == pl.num_programs(1) - 1)
    def _():
        z = DN_ALPHA * x_ref[...] + acc_ref[...]
        o_ref[...] = _layer_norm_rows(z, g_ref[...], b_ref[...])


def _ffn(x, w1, w3, w2, g, b):
    n = x.shape[0]
    tm = min(FFN_TILE, n)
    fc = FFN_CHUNK
    vec = pl.BlockSpec((1, D_MODEL), lambda i, j: (0, 0))
    return pl.pallas_call(
        _ffn_kernel,
        grid=(n // tm, D_FF // fc),
        in_specs=[pl.BlockSpec((tm, D_MODEL), lambda i, j: (i, 0)),
                  pl.BlockSpec((D_MODEL, fc), lambda i, j: (0, j)),
                  pl.BlockSpec((D_MODEL, fc), lambda i, j: (0, j)),
                  pl.BlockSpec((fc, D_MODEL), lambda i, j: (j, 0)),
                  vec, vec],
        out_specs=pl.BlockSpec((tm, D_MODEL), lambda i, j: (i, 0)),
        out_shape=jax.ShapeDtypeStruct((n, D_MODEL), F32),
        scratch_shapes=[pltpu.VMEM((tm, D_MODEL), BF16), pltpu.VMEM((tm, D_MODEL), F32)],
        compiler_params=_params("parallel", "arbitrary"),
        name="ffn_ln",
    )(x, w1, w3, w2, g, b)


MOE_TILE = 512
EXPERT_TILE = 512
GATE_LANES = 128
TOP_K = 2
INFO_P1, INFO_P2, INFO_R1, INFO_R2, INFO_E1, INFO_E2 = range(6)


def _router_kernel(x_ref, rw_ref, info_ref, count_ref, run_ref):
    step = pl.program_id(0)

    @pl.when(step == 0)
    def _():
        run_ref[...] = jnp.zeros_like(run_ref)

    x = x_ref[...]
    x_hi = x.astype(BF16)
    x_lo = (x - x_hi.astype(F32)).astype(BF16)
    part = (jnp.dot(x_hi, rw_ref[...], preferred_element_type=F32)
            + jnp.dot(x_lo, rw_ref[...], preferred_element_type=F32))
    logits = part + pltpu.roll(part, GATE_LANES - N_EXPERTS, 1)
    tm = logits.shape[0]
    lane = lax.broadcasted_iota(jnp.int32, logits.shape, 1)
    logits = jnp.where(lane < N_EXPERTS, logits, -jnp.inf)
    m1 = jnp.max(logits, axis=-1, keepdims=True)
    i1 = jnp.min(jnp.where(logits == m1, lane, GATE_LANES), axis=-1, keepdims=True)
    rest = jnp.where(lane == i1, -jnp.inf, logits)
    m2 = jnp.max(rest, axis=-1, keepdims=True)
    i2 = jnp.min(jnp.where(rest == m2, lane, GATE_LANES), axis=-1, keepdims=True)
    e2 = jnp.exp(m2 - m1)
    p1 = 1.0 / (1.0 + e2)
    p2 = e2 / (1.0 + e2)

    sel = jnp.logical_or(lane == i1, lane == i2)
    onehot = jnp.where(sel, 1.0, 0.0).astype(BF16)
    r_i = lax.broadcasted_iota(jnp.int32, (tm, tm), 0)
    c_i = lax.broadcasted_iota(jnp.int32, (tm, tm), 1)
    lower = jnp.where(c_i < r_i, 1.0, 0.0).astype(BF16)
    rank = jnp.dot(lower, onehot, preferred_element_type=F32) + run_ref[0:1, :]
    run_ref[...] = run_ref[...] + jnp.sum(onehot.astype(F32), axis=0, keepdims=True)
    r1 = jnp.sum(jnp.where(lane == i1, rank, 0.0), axis=-1, keepdims=True)
    r2 = jnp.sum(jnp.where(lane == i2, rank, 0.0), axis=-1, keepdims=True)

    info = jnp.where(lane == INFO_P1, p1, 0.0)
    info = jnp.where(lane == INFO_P2, p2, info)
    info = jnp.where(lane == INFO_R1, r1, info)
    info = jnp.where(lane == INFO_R2, r2, info)
    info = jnp.where(lane == INFO_E1, i1.astype(F32), info)
    info = jnp.where(lane == INFO_E2, i2.astype(F32), info)
    info_ref[...] = info
    count_ref[...] = run_ref[...]


def _router(x, rw):
    n = x.shape[0]
    tm = min(MOE_TILE, n)
    return pl.pallas_call(
        _router_kernel,
        grid=(n // tm,),
        in_specs=[pl.BlockSpec((tm, D_MODEL), lambda i: (i, 0)),
                  pl.BlockSpec((D_MODEL, GATE_LANES), lambda i: (0, 0))],
        out_specs=[pl.BlockSpec((tm, GATE_LANES), lambda i: (i, 0)),
                   pl.BlockSpec((SUBLANES, GATE_LANES), lambda i: (0, 0))],
        out_shape=[jax.ShapeDtypeStruct((n, GATE_LANES), F32),
                   jax.ShapeDtypeStruct((SUBLANES, GATE_LANES), F32)],
        scratch_shapes=[pltpu.VMEM((SUBLANES, GATE_LANES), F32)],
        compiler_params=_params("arbitrary"),
        name="moe_router",
    )(x, rw)


def _row_copy(src, src_row, dst, dst_row, sem):
    return pltpu.make_async_copy(src.at[pl.ds(src_row, 1)], dst.at[pl.ds(dst_row, 1)], sem)


def _dispatch_kernel(fill_ref, x_ref, pos_hbm, xs_hbm, pos_smem, zero_ref, pos_sem, row_sem):
    step = pl.program_id(0)
    tm = x_ref.shape[0]

    @pl.when(step == 0)
    def _():
        zero_ref[...] = jnp.zeros_like(zero_ref)

        def clear_tile(start):
            start = pl.multiple_of(start, EXPERT_TILE)
            fill = pltpu.make_async_copy(zero_ref, xs_hbm.at[pl.ds(start, EXPERT_TILE)], row_sem)
            fill.start()
            fill.wait()

        for e in range(N_EXPERTS):
            @pl.when(fill_ref[e] >= 0)
            def _():
                clear_tile(fill_ref[e])

        def clear_idle(t, _):
            clear_tile(t * EXPERT_TILE)
            return 0

        lax.fori_loop(fill_ref[N_EXPERTS], xs_hbm.shape[0] // EXPERT_TILE, clear_idle, 0)

    base = pl.multiple_of(step * (TOP_K * tm), TOP_K * tm)
    pos_copy = pltpu.make_async_copy(pos_hbm.at[pl.ds(base, TOP_K * tm)], pos_smem, pos_sem)
    pos_copy.start()
    pos_copy.wait()

    def issue(r, _):
        for k in range(TOP_K):
            _row_copy(x_ref, r, xs_hbm, pos_smem[TOP_K * r + k], row_sem).start()
        return 0

    lax.fori_loop(0, tm, issue, 0, unroll=ROW_DMA_UNROLL)
    for k in range(TOP_K):
        pltpu.make_async_copy(x_ref, xs_hbm.at[pl.ds(0, tm)], row_sem).wait()


def _dispatch(x, pos, fill_start, n_rows):
    n = x.shape[0]
    tm = min(MOE_TILE, n)
    return pl.pallas_call(
        _dispatch_kernel,
        grid_spec=pltpu.PrefetchScalarGridSpec(
            num_scalar_prefetch=1,
            grid=(n // tm,),
            in_specs=[pl.BlockSpec((tm, D_MODEL), lambda i, f: (i, 0)),
                      pl.BlockSpec(memory_space=pl.ANY)],
            out_specs=pl.BlockSpec(memory_space=pl.ANY),
            scratch_shapes=[pltpu.SMEM((TOP_K * tm,), jnp.int32),
                            pltpu.VMEM((EXPERT_TILE, D_MODEL), F32),
                            pltpu.SemaphoreType.DMA(()), pltpu.SemaphoreType.DMA(())]),
        out_shape=jax.ShapeDtypeStruct((n_rows, D_MODEL), F32),
        compiler_params=_params("arbitrary"),
        name="moe_dispatch",
    )(fill_start, x, pos)


def _experts_kernel(te_ref, nv_ref, xs_ref, w1_ref, w3_ref, w2_ref, ys_ref):
    i = pl.program_id(0)

    @pl.when(i < nv_ref[0])
    def _():
        xb = xs_ref[...].astype(BF16)
        h1 = jnp.dot(xb, w1_ref[0], preferred_element_type=F32)
        h3 = jnp.dot(xb, w3_ref[0], preferred_element_type=F32)
        h = (h1 * jax.nn.sigmoid(h1) * h3).astype(BF16)
        ys_ref[...] = jnp.dot(h, w2_ref[0], preferred_element_type=F32)

    @pl.when(i >= nv_ref[0])
    def _():
        ys_ref[...] = jnp.zeros_like(ys_ref)


def _experts(xs, tile_expert, n_valid, w1, w3, w2):
    n_rows = xs.shape[0]
    te = EXPERT_TILE
    wspec = lambda shape: pl.BlockSpec((1,) + shape, lambda i, t, nv: (t[i], 0, 0))
    return pl.pallas_call(
        _experts_kernel,
        grid_spec=pltpu.PrefetchScalarGridSpec(
            num_scalar_prefetch=2,
            grid=(n_rows // te,),
            in_specs=[pl.BlockSpec((te, D_MODEL), lambda i, t, nv: (jnp.minimum(i, nv[0] - 1), 0)),
                      wspec((D_MODEL, D_FF_EXPERT)), wspec((D_MODEL, D_FF_EXPERT)),
                      wspec((D_FF_EXPERT, D_MODEL))],
            out_specs=pl.BlockSpec((te, D_MODEL), lambda i, t, nv: (i, 0))),
        out_shape=jax.ShapeDtypeStruct((n_rows, D_MODEL), F32),
        compiler_params=_params("arbitrary"),
        name="moe_experts",
    )(tile_expert, n_valid, xs, w1, w3, w2)


def _combine_kernel(x_ref, info_ref, g_ref, b_ref, pos_hbm, ys_hbm, o_ref,
                    pos_smem, y1_ref, y2_ref, pos_sem, row_sem):
    step = pl.program_id(0)
    tm = x_ref.shape[0]
    base = pl.multiple_of(step * (TOP_K * tm), TOP_K * tm)
    pos_copy = pltpu.make_async_copy(pos_hbm.at[pl.ds(base, TOP_K * tm)], pos_smem, pos_sem)
    pos_copy.start()
    pos_copy.wait()

    def issue(r, _):
        _row_copy(ys_hbm, pos_smem[TOP_K * r], y1_ref, r, row_sem).start()
        _row_copy(ys_hbm, pos_smem[TOP_K * r + 1], y2_ref, r, row_sem).start()
        return 0

    lax.fori_loop(0, tm, issue, 0, unroll=ROW_DMA_UNROLL)
    for y_ref in (y1_ref, y2_ref):
        pltpu.make_async_copy(ys_hbm.at[pl.ds(0, tm)], y_ref, row_sem).wait()

    info = info_ref[...]
    p1 = info[:, INFO_P1:INFO_P1 + 1]
    p2 = info[:, INFO_P2:INFO_P2 + 1]
    z = DN_ALPHA * x_ref[...] + (p1 * y1_ref[...] + p2 * y2_ref[...])
    o_ref[...] = _layer_norm_rows(z, g_ref[...], b_ref[...])


def _combine(x, info, pos, ys, g, b):
    n = x.shape[0]
    tm = min(MOE_TILE, n)
    vec = pl.BlockSpec((1, D_MODEL), lambda i: (0, 0))
    return pl.pallas_call(
        _combine_kernel,
        grid=(n // tm,),
        in_specs=[pl.BlockSpec((tm, D_MODEL), lambda i: (i, 0)),
                  pl.BlockSpec((tm, GATE_LANES), lambda i: (i, 0)),
                  vec, vec,
                  pl.BlockSpec(memory_space=pl.ANY), pl.BlockSpec(memory_space=pl.ANY)],
        out_specs=pl.BlockSpec((tm, D_MODEL), lambda i: (i, 0)),
        out_shape=jax.ShapeDtypeStruct((n, D_MODEL), F32),
        scratch_shapes=[pltpu.SMEM((TOP_K * tm,), jnp.int32),
                        pltpu.VMEM((tm, D_MODEL), F32), pltpu.VMEM((tm, D_MODEL), F32),
                        pltpu.SemaphoreType.DMA(()), pltpu.SemaphoreType.DMA(())],
        compiler_params=_params("arbitrary"),
        name="moe_combine_ln",
    )(x, info, g, b, pos, ys)


def _moe(x, rw, w1, w3, w2, g, b):
    n = x.shape[0]
    te = EXPERT_TILE
    info, counts = _router(x, rw)
    cnt = counts[0, :N_EXPERTS].astype(jnp.int32)
    padded = ((cnt + te - 1) // te) * te
    ends = jnp.cumsum(padded)
    offsets = ends - padded
    experts = jnp.arange(N_EXPERTS, dtype=jnp.int32)

    def position(e_lane, r_lane):
        e = info[:, e_lane].astype(jnp.int32)
        r = info[:, r_lane].astype(jnp.int32)
        return jnp.sum(jnp.where(e[:, None] == experts[None, :], offsets[None, :], 0), axis=1) + r

    pos = jnp.stack([position(INFO_E1, INFO_R1), position(INFO_E2, INFO_R2)], axis=1)
    pos = pos.reshape(TOP_K * n)
    max_tiles = TOP_K * n // te + N_EXPERTS
    n_valid = (ends[-1] // te).astype(jnp.int32)
    tile_start = jnp.arange(max_tiles, dtype=jnp.int32) * te
    tile_start = jnp.minimum(tile_start, (n_valid - 1) * te)
    tile_expert = jnp.sum((tile_start[:, None] >= ends[None, :]).astype(jnp.int32), axis=1)
    fill_start = jnp.where(padded > 0, ends - te, -1).astype(jnp.int32)
    fill_start = jnp.concatenate([fill_start, n_valid.reshape(1)])

    xs = _dispatch(x, pos, fill_start, max_tiles * te)
    ys = _experts(xs, tile_expert, n_valid.reshape(1), w1, w3, w2)
    return _combine(x, info, pos, ys, g, b)


def _block_diag_gates(gate_w, gate_b):
    per = GATE_GROUP // LRU_BLOCK
    w = gate_w.reshape(2, N_GATE_GROUPS, per, LRU_BLOCK, LRU_BLOCK)
    eye = jnp.eye(per, dtype=gate_w.dtype)
    wbd = w[:, :, :, :, None, :] * eye[None, None, :, None, :, None]
    wbd = jnp.transpose(wbd, (1, 2, 3, 0, 4, 5)).reshape(N_GATE_GROUPS, GATE_GROUP, 2 * GATE_GROUP)
    b = gate_b.reshape(2, N_GATE_GROUPS, GATE_GROUP)
    b = jnp.transpose(b, (1, 0, 2)).reshape(N_GATE_GROUPS, 1, 2 * GATE_GROUP)
    return (0.5 * wbd).astype(BF16), (0.5 * b).astype(F32)


def _prepare_layer(l, w_in, conv_w, conv_b, lru_gate_w, lru_gate_b, lru_lambda, rpb, w_mem_kv,
                   mix_gain, w_out, ln1_g, ln1_b, ffn_w1, ffn_w3, ffn_w2, router_w, moe_w1,
                   moe_w3, moe_w2, ln2_g, ln2_b):
    p = {}
    p['w_in'] = w_in[l].astype(BF16)
    p['conv_w'] = conv_w[l].astype(F32)
    p['conv_b'] = conv_b[l].astype(F32).reshape(1, LRU_WIDTH)
    p['gates'] = [_block_diag_gates(lru_gate_w[l, d], lru_gate_b[l, d]) for d in range(2)]
    p['lam'] = [lru_lambda[l, d].astype(F32).reshape(1, LRU_WIDTH) for d in range(2)]
    p['na_bias'] = _natten_bias_table(rpb[l])
    p['w_mem_kv'] = w_mem_kv[l].astype(BF16)
    gain = mix_gain[l].astype(F32)
    p['gain_lru'] = gain[:LRU_WIDTH].reshape(1, LRU_WIDTH)
    p['gain_na'] = gain[LRU_WIDTH:LRU_WIDTH + NA_WIDTH].reshape(1, NA_WIDTH)
    p['gain_mem'] = gain[LRU_WIDTH + NA_WIDTH:].reshape(1, MEM_WIDTH)
    p['w_out'] = w_out[l].astype(BF16)
    vec = lambda v: v.astype(F32).reshape(1, D_MODEL)
    p['ln1'] = (vec(ln1_g[l]), vec(ln1_b[l]))
    p['ln2'] = (vec(ln2_g[l]), vec(ln2_b[l]))
    j = l // 2
    if l % 2 == 0:
        p['ffn'] = (ffn_w1[j].astype(BF16), ffn_w3[j].astype(BF16), ffn_w2[j].astype(BF16))
    else:
        w = router_w[j].astype(F32)
        w_hi = w.astype(BF16)
        w_lo = (w - w_hi.astype(F32)).astype(BF16)
        rw = jnp.zeros((D_MODEL, GATE_LANES), BF16)
        rw = rw.at[:, :N_EXPERTS].set(w_hi).at[:, N_EXPERTS:2 * N_EXPERTS].set(w_lo)
        p['moe'] = (rw, moe_w1[j].astype(BF16), moe_w3[j].astype(BF16), moe_w2[j].astype(BF16))
    return p


def _trunk(x, mem, layers):
    b, t, _ = x.shape
    n = b * t
    rows = t // GRID_W
    x = x.reshape(n, D_MODEL)
    mem2 = mem.reshape(b * N_MEM, D_MODEL)
    for p in layers:
        u_lru, u_att = _in_proj(x, p['w_in'])
        (wbd_f, gb_f), (wbd_b, gb_b) = p['gates']
        h_fwd = _lru_direction(u_lru, p['conv_w'], p['conv_b'], wbd_f, gb_f, p['lam'][0], t, False)
        y_lru = _lru_direction(u_lru, p['conv_w'], p['conv_b'], wbd_b, gb_b, p['lam'][1], t, True,
                               h_fwd=h_fwd, gain=p['gain_lru'])
        y_na = _natten(u_att.reshape(b, rows, GRID_W, 2048), p['na_bias'], p['gain_na'])
        kv = _matmul(mem2, p['w_mem_kv'], BF16).reshape(b, N_MEM, 2 * MEM_WIDTH)
        y_mem = _mem_attn(u_att.reshape(b, t, 2048), kv, p['gain_mem'])
        x = _out_proj(y_lru, y_na.reshape(n, NA_WIDTH), y_mem.reshape(n, MEM_WIDTH),
                      p['w_out'], x, *p['ln1'])
        if 'ffn' in p:
            x = _ffn(x, *p['ffn'], *p['ln2'])
        else:
            x = _moe(x, *p['moe'], *p['ln2'])
    return x.reshape(b, t, D_MODEL)


def kernel(x_prompt, x_sample, mem_prompt, mem_sample, w_in, conv_w, conv_b, lru_gate_w,
           lru_gate_b, lru_lambda, rpb, w_mem_kv, mix_gain, w_out, ln1_g, ln1_b, ffn_w1,
           ffn_w3, ffn_w2, router_w, moe_w1, moe_w3, moe_w2, ln2_g, ln2_b):
    weights = (w_in, conv_w, conv_b, lru_gate_w, lru_gate_b, lru_lambda, rpb, w_mem_kv,
               mix_gain, w_out, ln1_g, ln1_b, ffn_w1, ffn_w3, ffn_w2, router_w, moe_w1,
               moe_w3, moe_w2, ln2_g, ln2_b)
    layers = [_prepare_layer(l, *weights) for l in range(DEPTH)]
    return (_trunk(x_prompt, mem_prompt, layers), _trunk(x_sample, mem_sample, layers))
```

```python
import functools

import numpy as np
import jax
import jax.numpy as jnp
from jax import lax
from jax.experimental import pallas as pl
from jax.experimental.pallas import tpu as pltpu

F32 = jnp.float32
BF16 = jnp.bfloat16

D_MODEL = 2048
DEPTH = 4
GRID_W = 64
LRU_WIDTH = 1024
LRU_BLOCKS = 16
LRU_BLOCK = 64
LRU_C = 8.0
NA_WIDTH = 512
NA_HEADS = 8
NA_HEAD_DIM = 64
NA_WIN_ROWS = 8
NA_WIN_COLS = 16
MASK_VALUE = -1e30
MEM_WIDTH = 512
MEM_HEADS = 4
MEM_HEAD_DIM = 128
N_MEM = 256
IN_WIDTH = 4096
D_FF = 4096
N_EXPERTS = 8
D_FF_EXPERT = 1024
DN_ALPHA = (2 * DEPTH) ** 0.25
LN_EPS = 1e-5
RMS_EPS = 1e-6

VMEM_LIMIT_BYTES = 56 * 1024 * 1024
GATE_GROUP = 256
N_GATE_GROUPS = LRU_WIDTH // GATE_GROUP
SUBLANES = 8
ROW_DMA_UNROLL = 4


def _params(*semantics):
    return pltpu.CompilerParams(dimension_semantics=semantics,
                                vmem_limit_bytes=VMEM_LIMIT_BYTES)


def _layer_norm_rows(z, g, b):
    mu = jnp.mean(z, axis=-1, keepdims=True)
    zc = z - mu
    var = jnp.mean(zc * zc, axis=-1, keepdims=True)
    return zc * lax.rsqrt(var + LN_EPS) * g + b


def _rms_gain_rows(y, gain):
    ms = jnp.mean(y * y, axis=-1, keepdims=True)
    return y * lax.rsqrt(ms + RMS_EPS) * gain


def _in_proj_kernel(x_ref, w_ref, lru_ref, att_ref, xb_ref):
    j = pl.program_id(1)

    @pl.when(j == 0)
    def _():
        xb_ref[...] = x_ref[...].astype(BF16)

    y = jnp.dot(xb_ref[...], w_ref[...], preferred_element_type=F32)

    @pl.when(j < 2)
    def _():
        lru_ref[...] = y

    @pl.when(j >= 2)
    def _():
        att_ref[...] = y.astype(BF16)


def _in_proj(x, w):
    n = x.shape[0]
    tm = min(1024, n)
    tn = 1024
    return pl.pallas_call(
        _in_proj_kernel,
        grid=(n // tm, IN_WIDTH // tn),
        in_specs=[pl.BlockSpec((tm, D_MODEL), lambda i, j: (i, 0)),
                  pl.BlockSpec((D_MODEL, tn), lambda i, j: (0, j))],
        out_specs=[pl.BlockSpec((tm, tn), lambda i, j: (i, jnp.minimum(j, 1))),
                   pl.BlockSpec((tm, tn), lambda i, j: (i, jnp.maximum(j - 2, 0)))],
        out_shape=[jax.ShapeDtypeStruct((n, 2 * LRU_WIDTH), F32),
                   jax.ShapeDtypeStruct((n, 2048), BF16)],
        scratch_shapes=[pltpu.VMEM((tm, D_MODEL), BF16)],
        compiler_params=_params("parallel", "arbitrary"),
        name="in_proj",
    )(x, w)


def _matmul_kernel(x_ref, w_ref, o_ref):
    o_ref[...] = jnp.dot(x_ref[...].astype(BF16), w_ref[...],
                         preferred_element_type=F32).astype(o_ref.dtype)


def _matmul(x, w, out_dtype):
    m, k = x.shape
    n = w.shape[1]
    tm = min(512, m)
    return pl.pallas_call(
        _matmul_kernel,
        grid=(m // tm,),
        in_specs=[pl.BlockSpec((tm, k), lambda i: (i, 0)),
                  pl.BlockSpec((k, n), lambda i: (0, 0))],
        out_specs=pl.BlockSpec((tm, n), lambda i: (i, 0)),
        out_shape=jax.ShapeDtypeStruct((m, n), out_dtype),
        compiler_params=_params("parallel"),
        name="mem_kv_proj",
    )(x, w)


LRU_CHUNK = 512
LRU_SUB = 128
HALO = SUBLANES


def _gelu_tanh(x):
    return 0.5 * x * (1.0 + jnp.tanh(0.7978845608028654 * (x + 0.044715 * (x * x * x))))


def _lru_kernel(*refs, reverse, seq_len, n_chunks):
    if reverse:
        (x_ref, prev_ref, next_ref, cw_ref, cb_ref, wbd_ref, gb_ref, lam_ref,
         hf_ref, gl_ref, gain_ref, o_ref, a_ref, u_ref, carry_ref) = refs
        h_ref = u_ref
    else:
        (x_ref, prev_ref, next_ref, cw_ref, cb_ref, wbd_ref, gb_ref, lam_ref,
         o_ref, a_ref, u_ref, carry_ref) = refs
        h_ref = o_ref

    step = pl.program_id(0)
    chunk = (n_chunks - 1 - step) if reverse else step
    pos = (chunk * LRU_CHUNK) % seq_len
    seq_first = pos == 0
    seq_last = pos == seq_len - LRU_CHUNK

    halo_before = jnp.where(seq_first, 0.0, prev_ref[...])
    halo_after = jnp.where(seq_last, 0.0, next_ref[...])

    @pl.when(seq_last if reverse else seq_first)
    def _():
        carry_ref[...] = jnp.zeros_like(carry_ref)

    lam = lam_ref[...]
    neg = -lam
    softplus = jnp.maximum(neg, 0.0) + jnp.log(1.0 + jnp.exp(-jnp.abs(neg)))
    c_exp = (-0.5 * LRU_C * 1.4426950408889634) * softplus
    cw = cw_ref[...]
    cb = cb_ref[...]
    win = LRU_SUB + 2 * HALO

    def sub_block(s, _):
        r0 = pl.multiple_of(s * LRU_SUB, LRU_SUB)
        above = x_ref[pl.ds(pl.multiple_of(jnp.maximum(r0 - HALO, 0), HALO), HALO), :]
        below = x_ref[pl.ds(pl.multiple_of(jnp.minimum(r0 + LRU_SUB, LRU_CHUNK - HALO), HALO),
                            HALO), :]
        xw = jnp.concatenate([jnp.where(s == 0, halo_before, above),
                              x_ref[pl.ds(r0, LRU_SUB), :],
                              jnp.where(s == LRU_CHUNK // LRU_SUB - 1, halo_after, below)],
                             axis=0)
        xc = (cw[0:1] * pltpu.roll(xw, 2, 0) + cw[1:2] * pltpu.roll(xw, 1, 0)
              + cw[2:3] * xw + cw[3:4] * pltpu.roll(xw, win - 1, 0))
        xc = xc[HALO:HALO + LRU_SUB] + cb
        xcb = xc.astype(BF16)
        half_xc = 0.5 * xc
        for g in range(N_GATE_GROUPS):
            lo = g * GATE_GROUP
            half = jnp.dot(xcb[:, lo:lo + GATE_GROUP], wbd_ref[g],
                           preferred_element_type=F32) + gb_ref[g]
            t_r = jnp.tanh(half[:, :GATE_GROUP])
            t_i = jnp.tanh(half[:, GATE_GROUP:])
            cg = c_exp[:, lo:lo + GATE_GROUP]
            a = jnp.exp2(cg * t_r + cg)
            y = 1.0 - a * a
            mult = jnp.where(y > 0.0, y * lax.rsqrt(y), 0.0)
            a_ref[pl.ds(r0, LRU_SUB), lo:lo + GATE_GROUP] = a
            u_ref[pl.ds(r0, LRU_SUB), lo:lo + GATE_GROUP] = (
                (mult * half_xc[:, lo:lo + GATE_GROUP]) * (t_i + 1.0))
        return 0

    lax.fori_loop(0, LRU_CHUNK // LRU_SUB, sub_block, 0)

    row = lax.broadcasted_iota(jnp.int32, (SUBLANES, LRU_WIDTH), 0)
    n_blocks = LRU_CHUNK // SUBLANES

    def scan_block(k, carry):
        blk = (n_blocks - 1 - k) if reverse else k
        r0 = pl.multiple_of(blk * SUBLANES, SUBLANES)
        a = a_ref[pl.ds(r0, SUBLANES), :]
        u = u_ref[pl.ds(r0, SUBLANES), :]
        for d in (1, 2, 4):
            if reverse:
                keep = row < SUBLANES - d
                shift = SUBLANES - d
            else:
                keep = row >= d
                shift = d
            u = u + a * jnp.where(keep, pltpu.roll(u, shift, 0), 0.0)
            a = a * jnp.where(keep, pltpu.roll(a, shift, 0), 1.0)
        h = u + a * carry
        h_ref[pl.ds(r0, SUBLANES), :] = h
        edge = h[0:1, :] if reverse else h[SUBLANES - 1:SUBLANES, :]
        return jnp.broadcast_to(edge, (SUBLANES, LRU_WIDTH))

    carry_ref[...] = lax.fori_loop(0, n_blocks, scan_block, carry_ref[...], unroll=2)

    if reverse:
        gain = gain_ref[...]

        def out_block(s, _):
            r0 = pl.multiple_of(s * LRU_SUB, LRU_SUB)
            h = u_ref[pl.ds(r0, LRU_SUB), :] + hf_ref[pl.ds(r0, LRU_SUB), :]
            y = h * _gelu_tanh(gl_ref[pl.ds(r0, LRU_SUB), :])
            o_ref[pl.ds(r0, LRU_SUB), :] = _rms_gain_rows(y, gain).astype(BF16)
            return 0

        lax.fori_loop(0, LRU_CHUNK // LRU_SUB, out_block, 0)


def _lru_direction(u_lru, conv_w, conv_b, wbd, gate_b, lam, seq_len, reverse,
                   h_fwd=None, gain=None):
    n = u_lru.shape[0]
    n_chunks = n // LRU_CHUNK
    halo_per_chunk = LRU_CHUNK // HALO
    n_halo_blocks = n // HALO

    def cidx(s):
        return (n_chunks - 1 - s) if reverse else s

    def full(shape):
        return pl.BlockSpec(shape, lambda s: (0,) * len(shape))

    in_specs = [
        pl.BlockSpec((LRU_CHUNK, LRU_WIDTH), lambda s: (cidx(s), 0)),
        pl.BlockSpec((HALO, LRU_WIDTH),
                     lambda s: (jnp.maximum(cidx(s) * halo_per_chunk - 1, 0), 0)),
        pl.BlockSpec((HALO, LRU_WIDTH),
                     lambda s: (jnp.minimum((cidx(s) + 1) * halo_per_chunk, n_halo_blocks - 1), 0)),
        full((4, LRU_WIDTH)), full((1, LRU_WIDTH)),
        full((N_GATE_GROUPS, GATE_GROUP, 2 * GATE_GROUP)),
        full((N_GATE_GROUPS, 1, 2 * GATE_GROUP)),
        full((1, LRU_WIDTH)),
    ]
    args = [u_lru, u_lru, u_lru, conv_w, conv_b, wbd, gate_b, lam]
    if reverse:
        in_specs += [pl.BlockSpec((LRU_CHUNK, LRU_WIDTH), lambda s: (cidx(s), 0)),
                     pl.BlockSpec((LRU_CHUNK, LRU_WIDTH), lambda s: (cidx(s), 1)),
                     full((1, LRU_WIDTH))]
        args += [h_fwd, u_lru, gain]
    out_dtype = BF16 if reverse else F32
    return pl.pallas_call(
        functools.partial(_lru_kernel, reverse=reverse, seq_len=seq_len, n_chunks=n_chunks),
        grid=(n_chunks,),
        in_specs=in_specs,
        out_specs=pl.BlockSpec((LRU_CHUNK, LRU_WIDTH), lambda s: (cidx(s), 0)),
        out_shape=jax.ShapeDtypeStruct((n, LRU_WIDTH), out_dtype),
        scratch_shapes=[pltpu.VMEM((LRU_CHUNK, LRU_WIDTH), F32),
                        pltpu.VMEM((LRU_CHUNK, LRU_WIDTH), F32),
                        pltpu.VMEM((SUBLANES, LRU_WIDTH), F32)],
        compiler_params=_params("arbitrary"),
        name="lru_bwd" if reverse else "lru_fwd",
    )(*args)


NA_QROWS = 8
NA_KROWS = 16
NA_KBLK = 4
PAIR = 2 * NA_HEAD_DIM


def _natten_kernel(q_ref, k0, k1, k2, k3, v0, v1, v2, v3, bias_ref, gain_ref, o_ref,
                   ks_ref, vs_ref, s_ref, p_ref, *, rows):
    rblk = pl.program_id(1)
    for j, (kr, vr) in enumerate(((k0, v0), (k1, v1), (k2, v2), (k3, v3))):
        ks_ref[NA_KBLK * j:NA_KBLK * (j + 1)] = kr[0]
        vs_ref[NA_KBLK * j:NA_KBLK * (j + 1)] = vr[0]
    key_start = jnp.clip(rblk * NA_QROWS - NA_WIN_ROWS // 2, 0, rows - NA_KROWS)
    lane = lax.broadcasted_iota(jnp.int32, (GRID_W, PAIR), 1)
    lower = lane < NA_HEAD_DIM
    scale = NA_HEAD_DIM ** -0.5
    gain = gain_ref[...]

    def one_row(qr, _):
        r = rblk * NA_QROWS + qr
        win_start = jnp.clip(r - NA_WIN_ROWS // 2, 0, rows - NA_WIN_ROWS)
        d0 = win_start - key_start
        off = r - win_start
        q = q_ref[0, qr]
        pairs = range(NA_HEADS // 2)
        row_max, row_sum, outs = [], [], []
        for p in pairs:
            qp = q[:, p * PAIR:(p + 1) * PAIR] * scale
            q2 = jnp.concatenate([jnp.where(lower, qp, 0.0), jnp.where(lower, 0.0, qp)],
                                 axis=0).astype(BF16)
            kp = ks_ref[pl.ds(d0, NA_WIN_ROWS), :, p * PAIR:(p + 1) * PAIR]
            kp = kp.reshape(NA_WIN_ROWS * GRID_W, PAIR)
            s = lax.dot_general(q2, kp, (((1,), (1,)), ((), ())),
                                preferred_element_type=F32)
            s = s + bias_ref[off, p]
            s_ref[p] = s
            row_max.append(jnp.max(s, axis=-1, keepdims=True))
        for p in pairs:
            e = jnp.exp(s_ref[p] - row_max[p])
            row_sum.append(jnp.sum(e, axis=-1, keepdims=True))
            p_ref[p] = e.astype(BF16)
        for p in pairs:
            vp = vs_ref[pl.ds(d0, NA_WIN_ROWS), :, p * PAIR:(p + 1) * PAIR]
            vp = vp.reshape(NA_WIN_ROWS * GRID_W, PAIR)
            o = jnp.dot(p_ref[p], vp, preferred_element_type=F32) / row_sum[p]
            outs.append(jnp.where(lower, o[:GRID_W], o[GRID_W:]))
        y = jnp.concatenate(outs, axis=-1)
        o_ref[0, qr] = _rms_gain_rows(y, gain).astype(BF16)
        return 0

    lax.fori_loop(0, NA_QROWS, one_row, 0, unroll=True)


def _natten(u_att4, bias, gain):
    b, rows = u_att4.shape[0], u_att4.shape[1]
    n_kblk = rows // NA_KBLK

    def kv_spec(j, lane_block):
        def imap(bi, ri):
            start = jnp.clip(2 * ri - 1, 0, n_kblk - NA_KROWS // NA_KBLK)
            return (bi, start + j, 0, lane_block)
        return pl.BlockSpec((1, NA_KBLK, GRID_W, NA_WIDTH), imap)

    in_specs = ([pl.BlockSpec((1, NA_QROWS, GRID_W, NA_WIDTH), lambda bi, ri: (bi, ri, 0, 0))]
                + [kv_spec(j, 1) for j in range(4)] + [kv_spec(j, 2) for j in range(4)]
                + [pl.BlockSpec(bias.shape, lambda bi, ri: (0, 0, 0, 0)),
                   pl.BlockSpec((1, NA_WIDTH), lambda bi, ri: (0, 0))])
    return pl.pallas_call(
        functools.partial(_natten_kernel, rows=rows),
        grid=(b, rows // NA_QROWS),
        in_specs=in_specs,
        out_specs=pl.BlockSpec((1, NA_QROWS, GRID_W, NA_WIDTH), lambda bi, ri: (bi, ri, 0, 0)),
        out_shape=jax.ShapeDtypeStruct((b, rows, GRID_W, NA_WIDTH), BF16),
        scratch_shapes=[pltpu.VMEM((NA_KROWS, GRID_W, NA_WIDTH), BF16),
                        pltpu.VMEM((NA_KROWS, GRID_W, NA_WIDTH), BF16),
                        pltpu.VMEM((NA_HEADS // 2, 2 * GRID_W, NA_WIN_ROWS * GRID_W), F32),
                        pltpu.VMEM((NA_HEADS // 2, 2 * GRID_W, NA_WIN_ROWS * GRID_W), BF16)],
        compiler_params=_params("parallel", "parallel"),
        name="natten",
    )(*([u_att4] * 9), bias, gain)


def _natten_bias_table(rpb):
    off = np.arange(NA_WIN_ROWS)
    kr = np.arange(NA_WIN_ROWS)
    dr = kr[None, :] - off[:, None] + (NA_WIN_ROWS - 1)
    row_sel = (dr[:, :, None] == np.arange(2 * NA_WIN_ROWS - 1)).astype(np.float32)
    qc = np.arange(GRID_W)
    kc = np.arange(GRID_W)
    start = np.clip(qc - NA_WIN_COLS // 2, 0, GRID_W - NA_WIN_COLS)
    valid = (kc[None, :] >= start[:, None]) & (kc[None, :] < start[:, None] + NA_WIN_COLS)
    dc = kc[None, :] - qc[:, None] + (NA_WIN_COLS - 1)
    col_sel = ((dc[:, :, None] == np.arange(2 * NA_WIN_COLS - 1)) & valid[:, :, None])
    col_sel = col_sel.astype(np.float32)
    t = jnp.einsum('oka,hac,qjc->ohqkj', jnp.asarray(row_sel), rpb.astype(F32),
                   jnp.asarray(col_sel), precision=lax.Precision.HIGHEST)
    t = jnp.where(jnp.asarray(valid)[None, None, :, None, :], t, MASK_VALUE)
    return t.reshape(NA_WIN_ROWS, NA_HEADS // 2, 2 * GRID_W, NA_WIN_ROWS * GRID_W)


MEM_TILE = 512


def _mem_attn_kernel(q_ref, kv_ref, gain_ref, o_ref):
    q = q_ref[0]
    kv = kv_ref[0]
    scale = MEM_HEAD_DIM ** -0.5
    outs = []
    for h in range(MEM_HEADS):
        lo = h * MEM_HEAD_DIM
        s = lax.dot_general(q[:, lo:lo + MEM_HEAD_DIM], kv[:, lo:lo + MEM_HEAD_DIM],
                            (((1,), (1,)), ((), ())), preferred_element_type=F32) * scale
        m = jnp.max(s, axis=-1, keepdims=True)
        e = jnp.exp(s - m)
        l = jnp.sum(e, axis=-1, keepdims=True)
        v = kv[:, MEM_WIDTH + lo:MEM_WIDTH + lo + MEM_HEAD_DIM]
        outs.append(jnp.dot(e.astype(BF16), v, preferred_element_type=F32) / l)
    y = jnp.concatenate(outs, axis=-1)
    o_ref[0] = _rms_gain_rows(y, gain_ref[...]).astype(BF16)


def _mem_attn(u_att3, kv, gain):
    b, t = u_att3.shape[0], u_att3.shape[1]
    tm = min(MEM_TILE, t)
    return pl.pallas_call(
        _mem_attn_kernel,
        grid=(b, t // tm),
        in_specs=[pl.BlockSpec((1, tm, MEM_WIDTH), lambda bi, ti: (bi, ti, 3)),
                  pl.BlockSpec((1, N_MEM, 2 * MEM_WIDTH), lambda bi, ti: (bi, 0, 0)),
                  pl.BlockSpec((1, MEM_WIDTH), lambda bi, ti: (0, 0))],
        out_specs=pl.BlockSpec((1, tm, MEM_WIDTH), lambda bi, ti: (bi, ti, 0)),
        out_shape=jax.ShapeDtypeStruct((b, t, MEM_WIDTH), BF16),
        compiler_params=_params("parallel", "parallel"),
        name="mem_attn",
    )(u_att3, kv, gain)


OUT_TILE = 512


def _out_proj_kernel(yl_ref, yn_ref, ym_ref, w_ref, x_ref, g_ref, b_ref, o_ref):
    acc = jnp.dot(yl_ref[...], w_ref[0:LRU_WIDTH, :], preferred_element_type=F32)
    acc += jnp.dot(yn_ref[...], w_ref[LRU_WIDTH:LRU_WIDTH + NA_WIDTH, :],
                   preferred_element_type=F32)
    acc += jnp.dot(ym_ref[...], w_ref[LRU_WIDTH + NA_WIDTH:, :], preferred_element_type=F32)
    z = DN_ALPHA * x_ref[...] + acc
    o_ref[...] = _layer_norm_rows(z, g_ref[...], b_ref[...])


def _out_proj(y_lru, y_na, y_mem, w, x, g, b):
    n = x.shape[0]
    tm = min(OUT_TILE, n)
    row = lambda width: pl.BlockSpec((tm, width), lambda i: (i, 0))
    vec = pl.BlockSpec((1, D_MODEL), lambda i: (0, 0))
    return pl.pallas_call(
        _out_proj_kernel,
        grid=(n // tm,),
        in_specs=[row(LRU_WIDTH), row(NA_WIDTH), row(MEM_WIDTH),
                  pl.BlockSpec((D_MODEL, D_MODEL), lambda i: (0, 0)),
                  row(D_MODEL), vec, vec],
        out_specs=row(D_MODEL),
        out_shape=jax.ShapeDtypeStruct((n, D_MODEL), F32),
        compiler_params=_params("parallel"),
        name="out_proj_ln",
    )(y_lru, y_na, y_mem, w, x, g, b)


FFN_TILE = 512
FFN_CHUNK = 512


def _ffn_kernel(x_ref, w1_ref, w3_ref, w2_ref, g_ref, b_ref, o_ref, xb_ref, acc_ref):
    j = pl.program_id(1)

    @pl.when(j == 0)
    def _():
        xb_ref[...] = x_ref[...].astype(BF16)
        acc_ref[...] = jnp.zeros_like(acc_ref)

    xb = xb_ref[...]
    h1 = jnp.dot(xb, w1_ref[...], preferred_element_type=F32)
    h3 = jnp.dot(xb, w3_ref[...], preferred_element_type=F32)
    h = (h1 * jax.nn.sigmoid(h1) * h3).astype(BF16)
    acc_ref[...] += jnp.dot(h, w2_ref[...], preferred_element_type=F32)

    @pl.when(j == pl.num_programs(1) - 1)
    def _():
        z = DN_ALPHA * x_ref[...] + acc_ref[...]
        o_ref[...] = _layer_norm_rows(z, g_ref[...], b_ref[...])


def _ffn(x, w1, w3, w2, g, b):
    n = x.shape[0]
    tm = min(FFN_TILE, n)
    fc = FFN_CHUNK
    vec = pl.BlockSpec((1, D_MODEL), lambda i, j: (0, 0))
    return pl.pallas_call(
        _ffn_kernel,
        grid=(n // tm, D_FF // fc),
        in_specs=[pl.BlockSpec((tm, D_MODEL), lambda i, j: (i, 0)),
                  pl.BlockSpec((D_MODEL, fc), lambda i, j: (0, j)),
                  pl.BlockSpec((D_MODEL, fc), lambda i, j: (0, j)),
                  pl.BlockSpec((fc, D_MODEL), lambda i, j: (j, 0)),
                  vec, vec],
        out_specs=pl.BlockSpec((tm, D_MODEL), lambda i, j: (i, 0)),
        out_shape=jax.ShapeDtypeStruct((n, D_MODEL), F32),
        scratch_shapes=[pltpu.VMEM((tm, D_MODEL), BF16), pltpu.VMEM((tm, D_MODEL), F32)],
        compiler_params=_params("parallel", "arbitrary"),
        name="ffn_ln",
    )(x, w1, w3, w2, g, b)


MOE_TILE = 512
EXPERT_TILE = 512
GATE_LANES = 128
TOP_K = 2
INFO_P1, INFO_P2, INFO_R1, INFO_R2, INFO_E1, INFO_E2 = range(6)


def _router_kernel(x_ref, rw_ref, info_ref, count_ref, run_ref):
    step = pl.program_id(0)

    @pl.when(step == 0)
    def _():
        run_ref[...] = jnp.zeros_like(run_ref)

    x = x_ref[...]
    x_hi = x.astype(BF16)
    x_lo = (x - x_hi.astype(F32)).astype(BF16)
    part = (jnp.dot(x_hi, rw_ref[...], preferred_element_type=F32)
            + jnp.dot(x_lo, rw_ref[...], preferred_element_type=F32))
    logits = part + pltpu.roll(part, GATE_LANES - N_EXPERTS, 1)
    tm = logits.shape[0]
    lane = lax.broadcasted_iota(jnp.int32, logits.shape, 1)
    logits = jnp.where(lane < N_EXPERTS, logits, -jnp.inf)
    m1 = jnp.max(logits, axis=-1, keepdims=True)
    i1 = jnp.min(jnp.where(logits == m1, lane, GATE_LANES), axis=-1, keepdims=True)
    rest = jnp.where(lane == i1, -jnp.inf, logits)
    m2 = jnp.max(rest, axis=-1, keepdims=True)
    i2 = jnp.min(jnp.where(rest == m2, lane, GATE_LANES), axis=-1, keepdims=True)
    e2 = jnp.exp(m2 - m1)
    p1 = 1.0 / (1.0 + e2)
    p2 = e2 / (1.0 + e2)

    sel = jnp.logical_or(lane == i1, lane == i2)
    onehot = jnp.where(sel, 1.0, 0.0).astype(BF16)
    r_i = lax.broadcasted_iota(jnp.int32, (tm, tm), 0)
    c_i = lax.broadcasted_iota(jnp.int32, (tm, tm), 1)
    lower = jnp.where(c_i < r_i, 1.0, 0.0).astype(BF16)
    rank = jnp.dot(lower, onehot, preferred_element_type=F32) + run_ref[0:1, :]
    run_ref[...] = run_ref[...] + jnp.sum(onehot.astype(F32), axis=0, keepdims=True)
    r1 = jnp.sum(jnp.where(lane == i1, rank, 0.0), axis=-1, keepdims=True)
    r2 = jnp.sum(jnp.where(lane == i2, rank, 0.0), axis=-1, keepdims=True)

    info = jnp.where(lane == INFO_P1, p1, 0.0)
    info = jnp.where(lane == INFO_P2, p2, info)
    info = jnp.where(lane == INFO_R1, r1, info)
    info = jnp.where(lane == INFO_R2, r2, info)
    info = jnp.where(lane == INFO_E1, i1.astype(F32), info)
    info = jnp.where(lane == INFO_E2, i2.astype(F32), info)
    info_ref[...] = info
    count_ref[...] = run_ref[...]


def _router(x, rw):
    n = x.shape[0]
    tm = min(MOE_TILE, n)
    return pl.pallas_call(
        _router_kernel,
        grid=(n // tm,),
        in_specs=[pl.BlockSpec((tm, D_MODEL), lambda i: (i, 0)),
                  pl.BlockSpec((D_MODEL, GATE_LANES), lambda i: (0, 0))],
        out_specs=[pl.BlockSpec((tm, GATE_LANES), lambda i: (i, 0)),
                   pl.BlockSpec((SUBLANES, GATE_LANES), lambda i: (0, 0))],
        out_shape=[jax.ShapeDtypeStruct((n, GATE_LANES), F32),
                   jax.ShapeDtypeStruct((SUBLANES, GATE_LANES), F32)],
        scratch_shapes=[pltpu.VMEM((SUBLANES, GATE_LANES), F32)],
        compiler_params=_params("arbitrary"),
        name="moe_router",
    )(x, rw)


def _row_copy(src, src_row, dst, dst_row, sem):
    return pltpu.make_async_copy(src.at[pl.ds(src_row, 1)], dst.at[pl.ds(dst_row, 1)], sem)


def _dispatch_kernel(fill_ref, x_ref, pos_hbm, xs_hbm, pos_smem, zero_ref, pos_sem, row_sem):
    step = pl.program_id(0)
    tm = x_ref.shape[0]

    @pl.when(step == 0)
    def _():
        zero_ref[...] = jnp.zeros_like(zero_ref)

        def clear_tile(start):
            start = pl.multiple_of(start, EXPERT_TILE)
            fill = pltpu.make_async_copy(zero_ref, xs_hbm.at[pl.ds(start, EXPERT_TILE)], row_sem)
            fill.start()
            fill.wait()

        for e in range(N_EXPERTS):
            @pl.when(fill_ref[e] >= 0)
            def _():
                clear_tile(fill_ref[e])

        def clear_idle(t, _):
            clear_tile(t * EXPERT_TILE)
            return 0

        lax.fori_loop(fill_ref[N_EXPERTS], xs_hbm.shape[0] // EXPERT_TILE, clear_idle, 0)

    base = pl.multiple_of(step * (TOP_K * tm), TOP_K * tm)
    pos_copy = pltpu.make_async_copy(pos_hbm.at[pl.ds(base, TOP_K * tm)], pos_smem, pos_sem)
    pos_copy.start()
    pos_copy.wait()

    def issue(r, _):
        for k in range(TOP_K):
            _row_copy(x_ref, r, xs_hbm, pos_smem[TOP_K * r + k], row_sem).start()
        return 0

    lax.fori_loop(0, tm, issue, 0, unroll=ROW_DMA_UNROLL)
    for k in range(TOP_K):
        pltpu.make_async_copy(x_ref, xs_hbm.at[pl.ds(0, tm)], row_sem).wait()


def _dispatch(x, pos, fill_start, n_rows):
    n = x.shape[0]
    tm = min(MOE_TILE, n)
    return pl.pallas_call(
        _dispatch_kernel,
        grid_spec=pltpu.PrefetchScalarGridSpec(
            num_scalar_prefetch=1,
            grid=(n // tm,),
            in_specs=[pl.BlockSpec((tm, D_MODEL), lambda i, f: (i, 0)),
                      pl.BlockSpec(memory_space=pl.ANY)],
            out_specs=pl.BlockSpec(memory_space=pl.ANY),
            scratch_shapes=[pltpu.SMEM((TOP_K * tm,), jnp.int32),
                            pltpu.VMEM((EXPERT_TILE, D_MODEL), F32),
                            pltpu.SemaphoreType.DMA(()), pltpu.SemaphoreType.DMA(())]),
        out_shape=jax.ShapeDtypeStruct((n_rows, D_MODEL), F32),
        compiler_params=_params("arbitrary"),
        name="moe_dispatch",
    )(fill_start, x, pos)


def _experts_kernel(te_ref, nv_ref, xs_ref, w1_ref, w3_ref, w2_ref, ys_ref):
    i = pl.program_id(0)

    @pl.when(i < nv_ref[0])
    def _():
        xb = xs_ref[...].astype(BF16)
        h1 = jnp.dot(xb, w1_ref[0], preferred_element_type=F32)
        h3 = jnp.dot(xb, w3_ref[0], preferred_element_type=F32)
        h = (h1 * jax.nn.sigmoid(h1) * h3).astype(BF16)
        ys_ref[...] = jnp.dot(h, w2_ref[0], preferred_element_type=F32)

    @pl.when(i >= nv_ref[0])
    def _():
        ys_ref[...] = jnp.zeros_like(ys_ref)


def _experts(xs, tile_expert, n_valid, w1, w3, w2):
    n_rows = xs.shape[0]
    te = EXPERT_TILE
    wspec = lambda shape: pl.BlockSpec((1,) + shape, lambda i, t, nv: (t[i], 0, 0))
    return pl.pallas_call(
        _experts_kernel,
        grid_spec=pltpu.PrefetchScalarGridSpec(
            num_scalar_prefetch=2,
            grid=(n_rows // te,),
            in_specs=[pl.BlockSpec((te, D_MODEL), lambda i, t, nv: (jnp.minimum(i, nv[0] - 1), 0)),
                      wspec((D_MODEL, D_FF_EXPERT)), wspec((D_MODEL, D_FF_EXPERT)),
                      wspec((D_FF_EXPERT, D_MODEL))],
            out_specs=pl.BlockSpec((te, D_MODEL), lambda i, t, nv: (i, 0))),
        out_shape=jax.ShapeDtypeStruct((n_rows, D_MODEL), F32),
        compiler_params=_params("arbitrary"),
        name="moe_experts",
    )(tile_expert, n_valid, xs, w1, w3, w2)


COMBINE_ROWS = 64


def _combine_kernel(x_ref, info_ref, g_ref, b_ref, pos_hbm, ys_hbm, o_ref,
                    pos_smem, ya_ref, yb_ref, pos_sem, row_sem):
    step = pl.program_id(0)
    n_tiles = pl.num_programs(0) - 1
    tm = x_ref.shape[0]

    @pl.when(step == 0)
    def _():
        yb_ref[...] = jnp.zeros_like(yb_ref)

    tile = jnp.minimum(step, n_tiles - 1)
    base = pl.multiple_of(tile * (TOP_K * tm), TOP_K * tm)
    pos_copy = pltpu.make_async_copy(pos_hbm.at[pl.ds(base, TOP_K * tm)], pos_smem, pos_sem)
    pos_copy.start()
    pos_copy.wait()
    g = g_ref[...]
    b = b_ref[...]

    def wait_rows(y_ref, sem):
        for k in range(TOP_K):
            pltpu.make_async_copy(ys_hbm.at[pl.ds(0, tm)], y_ref.at[k], sem).wait()

    def run(gather_ref, gather_sem, norm_ref, norm_sem):
        @pl.when(step > 0)
        def _():
            wait_rows(norm_ref, norm_sem)

        def block(s, _):
            r0 = pl.multiple_of(s * COMBINE_ROWS, COMBINE_ROWS)
            for r in range(COMBINE_ROWS):
                for k in range(TOP_K):
                    _row_copy(ys_hbm, pos_smem[TOP_K * (r0 + r) + k], gather_ref.at[k], r0 + r,
                              gather_sem).start()
            rows = pl.ds(r0, COMBINE_ROWS)
            info = info_ref[rows, :]
            p1 = info[:, INFO_P1:INFO_P1 + 1]
            p2 = info[:, INFO_P2:INFO_P2 + 1]
            z = DN_ALPHA * x_ref[rows, :] + (p1 * norm_ref[0, rows, :] + p2 * norm_ref[1, rows, :])
            o_ref[rows, :] = _layer_norm_rows(z, g, b)
            return 0

        lax.fori_loop(0, tm // COMBINE_ROWS, block, 0)

        @pl.when(step == n_tiles)
        def _():
            wait_rows(gather_ref, gather_sem)

    @pl.when(step % 2 == 0)
    def _():
        run(ya_ref, row_sem.at[0], yb_ref, row_sem.at[1])

    @pl.when(step % 2 == 1)
    def _():
        run(yb_ref, row_sem.at[1], ya_ref, row_sem.at[0])


def _combine(x, info, pos, ys, g, b):
    n = x.shape[0]
    tm = min(MOE_TILE, n)
    lagged = lambda i: (jnp.maximum(i - 1, 0), 0)
    vec = pl.BlockSpec((1, D_MODEL), lambda i: (0, 0))
    return pl.pallas_call(
        _combine_kernel,
        grid=(n // tm + 1,),
        in_specs=[pl.BlockSpec((tm, D_MODEL), lagged),
                  pl.BlockSpec((tm, GATE_LANES), lagged),
                  vec, vec,
                  pl.BlockSpec(memory_space=pl.ANY), pl.BlockSpec(memory_space=pl.ANY)],
        out_specs=pl.BlockSpec((tm, D_MODEL), lagged),
        out_shape=jax.ShapeDtypeStruct((n, D_MODEL), F32),
        scratch_shapes=[pltpu.SMEM((TOP_K * tm,), jnp.int32),
                        pltpu.VMEM((TOP_K, tm, D_MODEL), F32),
                        pltpu.VMEM((TOP_K, tm, D_MODEL), F32),
                        pltpu.SemaphoreType.DMA(()), pltpu.SemaphoreType.DMA((2,))],
        compiler_params=_params("arbitrary"),
        name="moe_combine_ln",
    )(x, info, g, b, pos, ys)


def _moe(x, rw, w1, w3, w2, g, b):
    n = x.shape[0]
    te = EXPERT_TILE
    info, counts = _router(x, rw)
    cnt = counts[0, :N_EXPERTS].astype(jnp.int32)
    padded = ((cnt + te - 1) // te) * te
    ends = jnp.cumsum(padded)
    offsets = ends - padded
    experts = jnp.arange(N_EXPERTS, dtype=jnp.int32)

    def position(e_lane, r_lane):
        e = info[:, e_lane].astype(jnp.int32)
        r = info[:, r_lane].astype(jnp.int32)
        return jnp.sum(jnp.where(e[:, None] == experts[None, :], offsets[None, :], 0), axis=1) + r

    pos = jnp.stack([position(INFO_E1, INFO_R1), position(INFO_E2, INFO_R2)], axis=1)
    pos = pos.reshape(TOP_K * n)
    max_tiles = TOP_K * n // te + N_EXPERTS
    n_valid = (ends[-1] // te).astype(jnp.int32)
    tile_start = jnp.arange(max_tiles, dtype=jnp.int32) * te
    tile_start = jnp.minimum(tile_start, (n_valid - 1) * te)
    tile_expert = jnp.sum((tile_start[:, None] >= ends[None, :]).astype(jnp.int32), axis=1)
    fill_start = jnp.where(padded > 0, ends - te, -1).astype(jnp.int32)
    fill_start = jnp.concatenate([fill_start, n_valid.reshape(1)])

    xs = _dispatch(x, pos, fill_start, max_tiles * te)
    ys = _experts(xs, tile_expert, n_valid.reshape(1), w1, w3, w2)
    return _combine(x, info, pos, ys, g, b)


def _block_diag_gates(gate_w, gate_b):
    per = GATE_GROUP // LRU_BLOCK
    w = gate_w.reshape(2, N_GATE_GROUPS, per, LRU_BLOCK, LRU_BLOCK)
    eye = jnp.eye(per, dtype=gate_w.dtype)
    wbd = w[:, :, :, :, None, :] * eye[None, None, :, None, :, None]
    wbd = jnp.transpose(wbd, (1, 2, 3, 0, 4, 5)).reshape(N_GATE_GROUPS, GATE_GROUP, 2 * GATE_GROUP)
    b = gate_b.reshape(2, N_GATE_GROUPS, GATE_GROUP)
    b = jnp.transpose(b, (1, 0, 2)).reshape(N_GATE_GROUPS, 1, 2 * GATE_GROUP)
    return (0.5 * wbd).astype(BF16), (0.5 * b).astype(F32)


def _prepare_layer(l, w_in, conv_w, conv_b, lru_gate_w, lru_gate_b, lru_lambda, rpb, w_mem_kv,
                   mix_gain, w_out, ln1_g, ln1_b, ffn_w1, ffn_w3, ffn_w2, router_w, moe_w1,
                   moe_w3, moe_w2, ln2_g, ln2_b):
    p = {}
    p['w_in'] = w_in[l].astype(BF16)
    p['conv_w'] = conv_w[l].astype(F32)
    p['conv_b'] = conv_b[l].astype(F32).reshape(1, LRU_WIDTH)
    p['gates'] = [_block_diag_gates(lru_gate_w[l, d], lru_gate_b[l, d]) for d in range(2)]
    p['lam'] = [lru_lambda[l, d].astype(F32).reshape(1, LRU_WIDTH) for d in range(2)]
    p['na_bias'] = _natten_bias_table(rpb[l])
    p['w_mem_kv'] = w_mem_kv[l].astype(BF16)
    gain = mix_gain[l].astype(F32)
    p['gain_lru'] = gain[:LRU_WIDTH].reshape(1, LRU_WIDTH)
    p['gain_na'] = gain[LRU_WIDTH:LRU_WIDTH + NA_WIDTH].reshape(1, NA_WIDTH)
    p['gain_mem'] = gain[LRU_WIDTH + NA_WIDTH:].reshape(1, MEM_WIDTH)
    p['w_out'] = w_out[l].astype(BF16)
    vec = lambda v: v.astype(F32).reshape(1, D_MODEL)
    p['ln1'] = (vec(ln1_g[l]), vec(ln1_b[l]))
    p['ln2'] = (vec(ln2_g[l]), vec(ln2_b[l]))
    j = l // 2
    if l % 2 == 0:
        p['ffn'] = (ffn_w1[j].astype(BF16), ffn_w3[j].astype(BF16), ffn_w2[j].astype(BF16))
    else:
        w = router_w[j].astype(F32)
        w_hi = w.astype(BF16)
        w_lo = (w - w_hi.astype(F32)).astype(BF16)
        rw = jnp.zeros((D_MODEL, GATE_LANES), BF16)
        rw = rw.at[:, :N_EXPERTS].set(w_hi).at[:, N_EXPERTS:2 * N_EXPERTS].set(w_lo)
        p['moe'] = (rw, moe_w1[j].astype(BF16), moe_w3[j].astype(BF16), moe_w2[j].astype(BF16))
    return p


def _trunk(x, mem, layers):
    b, t, _ = x.shape
    n = b * t
    rows = t // GRID_W
    x = x.reshape(n, D_MODEL)
    mem2 = mem.reshape(b * N_MEM, D_MODEL)
    for p in layers:
        u_lru, u_att = _in_proj(x, p['w_in'])
        (wbd_f, gb_f), (wbd_b, gb_b) = p['gates']
        h_fwd = _lru_direction(u_lru, p['conv_w'], p['conv_b'], wbd_f, gb_f, p['lam'][0], t, False)
        y_lru = _lru_direction(u_lru, p['conv_w'], p['conv_b'], wbd_b, gb_b, p['lam'][1], t, True,
                               h_fwd=h_fwd, gain=p['gain_lru'])
        y_na = _natten(u_att.reshape(b, rows, GRID_W, 2048), p['na_bias'], p['gain_na'])
        kv = _matmul(mem2, p['w_mem_kv'], BF16).reshape(b, N_MEM, 2 * MEM_WIDTH)
        y_mem = _mem_attn(u_att.reshape(b, t, 2048), kv, p['gain_mem'])
        x = _out_proj(y_lru, y_na.reshape(n, NA_WIDTH), y_mem.reshape(n, MEM_WIDTH),
                      p['w_out'], x, *p['ln1'])
        if 'ffn' in p:
            x = _ffn(x, *p['ffn'], *p['ln2'])
        else:
            x = _moe(x, *p['moe'], *p['ln2'])
    return x.reshape(b, t, D_MODEL)


def kernel(x_prompt, x_sample, mem_prompt, mem_sample, w_in, conv_w, conv_b, lru_gate_w,
           lru_gate_b, lru_lambda, rpb, w_mem_kv, mix_gain, w_out, ln1_g, ln1_b, ffn_w1,
           ffn_w3, ffn_w2, router_w, moe_w1, moe_w3, moe_w2, ln2_g, ln2_b):
    weights = (w_in, conv_w, conv_b, lru_gate_w, lru_gate_b, lru_lambda, rpb, w_mem_kv,
               mix_gain, w_out, ln1_g, ln1_b, ffn_w1, ffn_w3, ffn_w2, router_w, moe_w1,
               moe_w3, moe_w2, ln2_g, ln2_b)
    layers = [_prepare_layer(l, *weights) for l in range(DEPTH)]
    return (_trunk(x_prompt, mem_prompt, layers), _trunk(x_sample, mem_sample, layers))
```

```python
import functools

import numpy as np
import jax
import jax.numpy as jnp
from jax import lax
from jax.experimental import pallas as pl
from jax.experimental.pallas import tpu as pltpu

F32 = jnp.float32
BF16 = jnp.bfloat16

D_MODEL = 2048
DEPTH = 4
GRID_W = 64
LRU_WIDTH = 1024
LRU_BLOCKS = 16
LRU_BLOCK = 64
LRU_C = 8.0
NA_WIDTH = 512
NA_HEADS = 8
NA_HEAD_DIM = 64
NA_WIN_ROWS = 8
NA_WIN_COLS = 16
MASK_VALUE = -1e30
MEM_WIDTH = 512
MEM_HEADS = 4
MEM_HEAD_DIM = 128
N_MEM = 256
IN_WIDTH = 4096
D_FF = 4096
N_EXPERTS = 8
D_FF_EXPERT = 1024
DN_ALPHA = (2 * DEPTH) ** 0.25
LN_EPS = 1e-5
RMS_EPS = 1e-6

VMEM_LIMIT_BYTES = 56 * 1024 * 1024
GATE_GROUP = 256
N_GATE_GROUPS = LRU_WIDTH // GATE_GROUP
SUBLANES = 8
ROW_DMA_UNROLL = 4


def _params(*semantics):
    return pltpu.CompilerParams(dimension_semantics=semantics,
                                vmem_limit_bytes=VMEM_LIMIT_BYTES)


def _layer_norm_rows(z, g, b):
    mu = jnp.mean(z, axis=-1, keepdims=True)
    zc = z - mu
    var = jnp.mean(zc * zc, axis=-1, keepdims=True)
    return zc * lax.rsqrt(var + LN_EPS) * g + b


def _rms_gain_rows(y, gain):
    ms = jnp.mean(y * y, axis=-1, keepdims=True)
    return y * lax.rsqrt(ms + RMS_EPS) * gain


def _in_proj_kernel(x_ref, w_ref, lru_ref, att_ref, xb_ref):
    j = pl.program_id(1)

    @pl.when(j == 0)
    def _():
        xb_ref[...] = x_ref[...].astype(BF16)

    y = jnp.dot(xb_ref[...], w_ref[...], preferred_element_type=F32)

    @pl.when(j < 2)
    def _():
        lru_ref[...] = y

    @pl.when(j >= 2)
    def _():
        att_ref[...] = y.astype(BF16)


def _in_proj(x, w):
    n = x.shape[0]
    tm = min(1024, n)
    tn = 1024
    return pl.pallas_call(
        _in_proj_kernel,
        grid=(n // tm, IN_WIDTH // tn),
        in_specs=[pl.BlockSpec((tm, D_MODEL), lambda i, j: (i, 0)),
                  pl.BlockSpec((D_MODEL, tn), lambda i, j: (0, j))],
        out_specs=[pl.BlockSpec((tm, tn), lambda i, j: (i, jnp.minimum(j, 1))),
                   pl.BlockSpec((tm, tn), lambda i, j: (i, jnp.maximum(j - 2, 0)))],
        out_shape=[jax.ShapeDtypeStruct((n, 2 * LRU_WIDTH), F32),
                   jax.ShapeDtypeStruct((n, 2048), BF16)],
        scratch_shapes=[pltpu.VMEM((tm, D_MODEL), BF16)],
        compiler_params=_params("parallel", "arbitrary"),
        name="in_proj",
    )(x, w)


def _matmul_kernel(x_ref, w_ref, o_ref):
    o_ref[...] = jnp.dot(x_ref[...].astype(BF16), w_ref[...],
                         preferred_element_type=F32).astype(o_ref.dtype)


def _matmul(x, w, out_dtype):
    m, k = x.shape
    n = w.shape[1]
    tm = min(512, m)
    return pl.pallas_call(
        _matmul_kernel,
        grid=(m // tm,),
        in_specs=[pl.BlockSpec((tm, k), lambda i: (i, 0)),
                  pl.BlockSpec((k, n), lambda i: (0, 0))],
        out_specs=pl.BlockSpec((tm, n), lambda i: (i, 0)),
        out_shape=jax.ShapeDtypeStruct((m, n), out_dtype),
        compiler_params=_params("parallel"),
        name="mem_kv_proj",
    )(x, w)


LRU_CHUNK = 512
LRU_SUB = 128
HALO = SUBLANES


def _gelu_tanh(x):
    return 0.5 * x * (1.0 + jnp.tanh(0.7978845608028654 * (x + 0.044715 * (x * x * x))))


def _lru_kernel(*refs, reverse, seq_len, n_chunks):
    if reverse:
        (xc_ref, wbd_ref, gb_ref, lam_ref,
         hf_ref, gl_ref, gain_ref, o_ref, a_ref, u_ref, carry_ref) = refs
        h_ref = u_ref
    else:
        (x_ref, prev_ref, next_ref, cw_ref, cb_ref, wbd_ref, gb_ref, lam_ref,
         o_ref, xc_ref, a_ref, u_ref, carry_ref) = refs
        h_ref = o_ref

    step = pl.program_id(0)
    chunk = (n_chunks - 1 - step) if reverse else step
    pos = (chunk * LRU_CHUNK) % seq_len
    seq_first = pos == 0
    seq_last = pos == seq_len - LRU_CHUNK

    @pl.when(seq_last if reverse else seq_first)
    def _():
        carry_ref[...] = jnp.zeros_like(carry_ref)

    lam = lam_ref[...]
    neg = -lam
    softplus = jnp.maximum(neg, 0.0) + jnp.log(1.0 + jnp.exp(-jnp.abs(neg)))
    c_exp = (-0.5 * LRU_C * 1.4426950408889634) * softplus
    win = LRU_SUB + 2 * HALO

    def conv_rows(s, r0):
        halo_before = jnp.where(seq_first, 0.0, prev_ref[...])
        halo_after = jnp.where(seq_last, 0.0, next_ref[...])
        cw = cw_ref[...]
        above = x_ref[pl.ds(pl.multiple_of(jnp.maximum(r0 - HALO, 0), HALO), HALO), :]
        below = x_ref[pl.ds(pl.multiple_of(jnp.minimum(r0 + LRU_SUB, LRU_CHUNK - HALO), HALO),
                            HALO), :]
        xw = jnp.concatenate([jnp.where(s == 0, halo_before, above),
                              x_ref[pl.ds(r0, LRU_SUB), :],
                              jnp.where(s == LRU_CHUNK // LRU_SUB - 1, halo_after, below)],
                             axis=0)
        xc = (cw[0:1] * pltpu.roll(xw, 2, 0) + cw[1:2] * pltpu.roll(xw, 1, 0)
              + cw[2:3] * xw + cw[3:4] * pltpu.roll(xw, win - 1, 0))
        return xc[HALO:HALO + LRU_SUB] + cb_ref[...]

    def sub_block(s, _):
        r0 = pl.multiple_of(s * LRU_SUB, LRU_SUB)
        if reverse:
            xc = xc_ref[pl.ds(r0, LRU_SUB), :]
        else:
            xc = conv_rows(s, r0)
            xc_ref[pl.ds(r0, LRU_SUB), :] = xc
        xcb = xc.astype(BF16)
        half_xc = 0.5 * xc
        for g in range(N_GATE_GROUPS):
            lo = g * GATE_GROUP
            half = jnp.dot(xcb[:, lo:lo + GATE_GROUP], wbd_ref[g],
                           preferred_element_type=F32) + gb_ref[g]
            t_r = jnp.tanh(half[:, :GATE_GROUP])
            t_i = jnp.tanh(half[:, GATE_GROUP:])
            cg = c_exp[:, lo:lo + GATE_GROUP]
            a = jnp.exp2(cg * t_r + cg)
            y = 1.0 - a * a
            mult = jnp.where(y > 0.0, y * lax.rsqrt(y), 0.0)
            a_ref[pl.ds(r0, LRU_SUB), lo:lo + GATE_GROUP] = a
            u_ref[pl.ds(r0, LRU_SUB), lo:lo + GATE_GROUP] = (
                (mult * half_xc[:, lo:lo + GATE_GROUP]) * (t_i + 1.0))
        return 0

    lax.fori_loop(0, LRU_CHUNK // LRU_SUB, sub_block, 0)

    row = lax.broadcasted_iota(jnp.int32, (SUBLANES, LRU_WIDTH), 0)
    n_blocks = LRU_CHUNK // SUBLANES

    def scan_block(k, carry):
        blk = (n_blocks - 1 - k) if reverse else k
        r0 = pl.multiple_of(blk * SUBLANES, SUBLANES)
        a = a_ref[pl.ds(r0, SUBLANES), :]
        u = u_ref[pl.ds(r0, SUBLANES), :]
        for d in (1, 2, 4):
            if reverse:
                keep = row < SUBLANES - d
                shift = SUBLANES - d
            else:
                keep = row >= d
                shift = d
            u = u + a * jnp.where(keep, pltpu.roll(u, shift, 0), 0.0)
            a = a * jnp.where(keep, pltpu.roll(a, shift, 0), 1.0)
        h = u + a * carry
        h_ref[pl.ds(r0, SUBLANES), :] = h
        edge = h[0:1, :] if reverse else h[SUBLANES - 1:SUBLANES, :]
        return jnp.broadcast_to(edge, (SUBLANES, LRU_WIDTH))

    carry_ref[...] = lax.fori_loop(0, n_blocks, scan_block, carry_ref[...], unroll=4)

    if reverse:
        gain = gain_ref[...]

        def out_block(s, _):
            r0 = pl.multiple_of(s * LRU_SUB, LRU_SUB)
            h = u_ref[pl.ds(r0, LRU_SUB), :] + hf_ref[pl.ds(r0, LRU_SUB), :]
            y = h * _gelu_tanh(gl_ref[pl.ds(r0, LRU_SUB), :])
            o_ref[pl.ds(r0, LRU_SUB), :] = _rms_gain_rows(y, gain).astype(BF16)
            return 0

        lax.fori_loop(0, LRU_CHUNK // LRU_SUB, out_block, 0)


def _lru_direction(u_lru, conv_w, conv_b, wbd, gate_b, lam, seq_len, reverse,
                   xc=None, h_fwd=None, gain=None):
    n = u_lru.shape[0]
    n_chunks = n // LRU_CHUNK
    halo_per_chunk = LRU_CHUNK // HALO
    n_halo_blocks = n // HALO

    def cidx(s):
        return (n_chunks - 1 - s) if reverse else s

    def full(shape):
        return pl.BlockSpec(shape, lambda s: (0,) * len(shape))

    chunk_spec = pl.BlockSpec((LRU_CHUNK, LRU_WIDTH), lambda s: (cidx(s), 0))
    gate_specs = [full((N_GATE_GROUPS, GATE_GROUP, 2 * GATE_GROUP)),
                  full((N_GATE_GROUPS, 1, 2 * GATE_GROUP)),
                  full((1, LRU_WIDTH))]
    if reverse:
        in_specs = [chunk_spec] + gate_specs + [
            chunk_spec, pl.BlockSpec((LRU_CHUNK, LRU_WIDTH), lambda s: (cidx(s), 1)),
            full((1, LRU_WIDTH))]
        args = [xc, wbd, gate_b, lam, h_fwd, u_lru, gain]
        out_specs = chunk_spec
        out_shape = jax.ShapeDtypeStruct((n, LRU_WIDTH), BF16)
    else:
        in_specs = [
            chunk_spec,
            pl.BlockSpec((HALO, LRU_WIDTH),
                         lambda s: (jnp.maximum(cidx(s) * halo_per_chunk - 1, 0), 0)),
            pl.BlockSpec((HALO, LRU_WIDTH),
                         lambda s: (jnp.minimum((cidx(s) + 1) * halo_per_chunk,
                                                n_halo_blocks - 1), 0)),
            full((4, LRU_WIDTH)), full((1, LRU_WIDTH))] + gate_specs
        args = [u_lru, u_lru, u_lru, conv_w, conv_b, wbd, gate_b, lam]
        out_specs = [chunk_spec, chunk_spec]
        out_shape = [jax.ShapeDtypeStruct((n, LRU_WIDTH), F32)] * 2
    return pl.pallas_call(
        functools.partial(_lru_kernel, reverse=reverse, seq_len=seq_len, n_chunks=n_chunks),
        grid=(n_chunks,),
        in_specs=in_specs,
        out_specs=out_specs,
        out_shape=out_shape,
        scratch_shapes=[pltpu.VMEM((LRU_CHUNK, LRU_WIDTH), F32),
                        pltpu.VMEM((LRU_CHUNK, LRU_WIDTH), F32),
                        pltpu.VMEM((SUBLANES, LRU_WIDTH), F32)],
        compiler_params=_params("arbitrary"),
        name="lru_bwd" if reverse else "lru_fwd",
    )(*args)


NA_QROWS = 8
NA_KROWS = 16
NA_KBLK = 4
PAIR = 2 * NA_HEAD_DIM


def _natten_kernel(q_ref, k0, k1, k2, k3, v0, v1, v2, v3, bias_ref, gain_ref, o_ref,
                   ks_ref, vs_ref, s_ref, p_ref, *, rows):
    rblk = pl.program_id(1)
    for j, (kr, vr) in enumerate(((k0, v0), (k1, v1), (k2, v2), (k3, v3))):
        ks_ref[NA_KBLK * j:NA_KBLK * (j + 1)] = kr[0]
        vs_ref[NA_KBLK * j:NA_KBLK * (j + 1)] = vr[0]
    key_start = jnp.clip(rblk * NA_QROWS - NA_WIN_ROWS // 2, 0, rows - NA_KROWS)
    lane = lax.broadcasted_iota(jnp.int32, (GRID_W, PAIR), 1)
    lower = lane < NA_HEAD_DIM
    scale = NA_HEAD_DIM ** -0.5
    gain = gain_ref[...]

    def one_row(qr, _):
        r = rblk * NA_QROWS + qr
        win_start = jnp.clip(r - NA_WIN_ROWS // 2, 0, rows - NA_WIN_ROWS)
        d0 = win_start - key_start
        off = r - win_start
        q = q_ref[0, qr]
        pairs = range(NA_HEADS // 2)
        row_max, row_sum, outs = [], [], []
        for p in pairs:
            qp = q[:, p * PAIR:(p + 1) * PAIR] * scale
            q2 = jnp.concatenate([jnp.where(lower, qp, 0.0), jnp.where(lower, 0.0, qp)],
                                 axis=0).astype(BF16)
            kp = ks_ref[pl.ds(d0, NA_WIN_ROWS), :, p * PAIR:(p + 1) * PAIR]
            kp = kp.reshape(NA_WIN_ROWS * GRID_W, PAIR)
            s = lax.dot_general(q2, kp, (((1,), (1,)), ((), ())),
                                preferred_element_type=F32)
            s = s + bias_ref[off, p]
            s_ref[p] = s
            row_max.append(jnp.max(s, axis=-1, keepdims=True))
        for p in pairs:
            e = jnp.exp(s_ref[p] - row_max[p])
            row_sum.append(jnp.sum(e, axis=-1, keepdims=True))
            p_ref[p] = e.astype(BF16)
        for p in pairs:
            vp = vs_ref[pl.ds(d0, NA_WIN_ROWS), :, p * PAIR:(p + 1) * PAIR]
            vp = vp.reshape(NA_WIN_ROWS * GRID_W, PAIR)
            o = jnp.dot(p_ref[p], vp, preferred_element_type=F32) / row_sum[p]
            outs.append(jnp.where(lower, o[:GRID_W], o[GRID_W:]))
        y = jnp.concatenate(outs, axis=-1)
        o_ref[0, qr] = _rms_gain_rows(y, gain).astype(BF16)
        return 0

    lax.fori_loop(0, NA_QROWS, one_row, 0, unroll=True)


def _natten(u_att4, bias, gain):
    b, rows = u_att4.shape[0], u_att4.shape[1]
    n_kblk = rows // NA_KBLK

    def kv_spec(j, lane_block):
        def imap(bi, ri):
            start = jnp.clip(2 * ri - 1, 0, n_kblk - NA_KROWS // NA_KBLK)
            return (bi, start + j, 0, lane_block)
        return pl.BlockSpec((1, NA_KBLK, GRID_W, NA_WIDTH), imap)

    in_specs = ([pl.BlockSpec((1, NA_QROWS, GRID_W, NA_WIDTH), lambda bi, ri: (bi, ri, 0, 0))]
                + [kv_spec(j, 1) for j in range(4)] + [kv_spec(j, 2) for j in range(4)]
                + [pl.BlockSpec(bias.shape, lambda bi, ri: (0, 0, 0, 0)),
                   pl.BlockSpec((1, NA_WIDTH), lambda bi, ri: (0, 0))])
    return pl.pallas_call(
        functools.partial(_natten_kernel, rows=rows),
        grid=(b, rows // NA_QROWS),
        in_specs=in_specs,
        out_specs=pl.BlockSpec((1, NA_QROWS, GRID_W, NA_WIDTH), lambda bi, ri: (bi, ri, 0, 0)),
        out_shape=jax.ShapeDtypeStruct((b, rows, GRID_W, NA_WIDTH), BF16),
        scratch_shapes=[pltpu.VMEM((NA_KROWS, GRID_W, NA_WIDTH), BF16),
                        pltpu.VMEM((NA_KROWS, GRID_W, NA_WIDTH), BF16),
                        pltpu.VMEM((NA_HEADS // 2, 2 * GRID_W, NA_WIN_ROWS * GRID_W), F32),
                        pltpu.VMEM((NA_HEADS // 2, 2 * GRID_W, NA_WIN_ROWS * GRID_W), BF16)],
        compiler_params=_params("parallel", "parallel"),
        name="natten",
    )(*([u_att4] * 9), bias, gain)


def _natten_bias_table(rpb):
    off = np.arange(NA_WIN_ROWS)
    kr = np.arange(NA_WIN_ROWS)
    dr = kr[None, :] - off[:, None] + (NA_WIN_ROWS - 1)
    row_sel = (dr[:, :, None] == np.arange(2 * NA_WIN_ROWS - 1)).astype(np.float32)
    qc = np.arange(GRID_W)
    kc = np.arange(GRID_W)
    start = np.clip(qc - NA_WIN_COLS // 2, 0, GRID_W - NA_WIN_COLS)
    valid = (kc[None, :] >= start[:, None]) & (kc[None, :] < start[:, None] + NA_WIN_COLS)
    dc = kc[None, :] - qc[:, None] + (NA_WIN_COLS - 1)
    col_sel = ((dc[:, :, None] == np.arange(2 * NA_WIN_COLS - 1)) & valid[:, :, None])
    col_sel = col_sel.astype(np.float32)
    t = jnp.einsum('oka,hac,qjc->ohqkj', jnp.asarray(row_sel), rpb.astype(F32),
                   jnp.asarray(col_sel), precision=lax.Precision.HIGHEST)
    t = jnp.where(jnp.asarray(valid)[None, None, :, None, :], t, MASK_VALUE)
    return t.reshape(NA_WIN_ROWS, NA_HEADS // 2, 2 * GRID_W, NA_WIN_ROWS * GRID_W)


MEM_TILE = 512


def _mem_attn_kernel(q_ref, kv_ref, gain_ref, o_ref):
    q = q_ref[0]
    kv = kv_ref[0]
    scale = MEM_HEAD_DIM ** -0.5
    outs = []
    for h in range(MEM_HEADS):
        lo = h * MEM_HEAD_DIM
        s = lax.dot_general(q[:, lo:lo + MEM_HEAD_DIM], kv[:, lo:lo + MEM_HEAD_DIM],
                            (((1,), (1,)), ((), ())), preferred_element_type=F32) * scale
        m = jnp.max(s, axis=-1, keepdims=True)
        e = jnp.exp(s - m)
        l = jnp.sum(e, axis=-1, keepdims=True)
        v = kv[:, MEM_WIDTH + lo:MEM_WIDTH + lo + MEM_HEAD_DIM]
        outs.append(jnp.dot(e.astype(BF16), v, preferred_element_type=F32) / l)
    y = jnp.concatenate(outs, axis=-1)
    o_ref[0] = _rms_gain_rows(y, gain_ref[...]).astype(BF16)


def _mem_attn(u_att3, kv, gain):
    b, t = u_att3.shape[0], u_att3.shape[1]
    tm = min(MEM_TILE, t)
    return pl.pallas_call(
        _mem_attn_kernel,
        grid=(b, t // tm),
        in_specs=[pl.BlockSpec((1, tm, MEM_WIDTH), lambda bi, ti: (bi, ti, 3)),
                  pl.BlockSpec((1, N_MEM, 2 * MEM_WIDTH), lambda bi, ti: (bi, 0, 0)),
                  pl.BlockSpec((1, MEM_WIDTH), lambda bi, ti: (0, 0))],
        out_specs=pl.BlockSpec((1, tm, MEM_WIDTH), lambda bi, ti: (bi, ti, 0)),
        out_shape=jax.ShapeDtypeStruct((b, t, MEM_WIDTH), BF16),
        compiler_params=_params("parallel", "parallel"),
        name="mem_attn",
    )(u_att3, kv, gain)


OUT_TILE = 512


def _out_proj_kernel(yl_ref, yn_ref, ym_ref, w_ref, x_ref, g_ref, b_ref, o_ref):
    acc = jnp.dot(yl_ref[...], w_ref[0:LRU_WIDTH, :], preferred_element_type=F32)
    acc += jnp.dot(yn_ref[...], w_ref[LRU_WIDTH:LRU_WIDTH + NA_WIDTH, :],
                   preferred_element_type=F32)
    acc += jnp.dot(ym_ref[...], w_ref[LRU_WIDTH + NA_WIDTH:, :], preferred_element_type=F32)
    z = DN_ALPHA * x_ref[...] + acc
    o_ref[...] = _layer_norm_rows(z, g_ref[...], b_ref[...])


def _out_proj(y_lru, y_na, y_mem, w, x, g, b):
    n = x.shape[0]
    tm = min(OUT_TILE, n)
    row = lambda width: pl.BlockSpec((tm, width), lambda i: (i, 0))
    vec = pl.BlockSpec((1, D_MODEL), lambda i: (0, 0))
    return pl.pallas_call(
        _out_proj_kernel,
        grid=(n // tm,),
        in_specs=[row(LRU_WIDTH), row(NA_WIDTH), row(MEM_WIDTH),
                  pl.BlockSpec((D_MODEL, D_MODEL), lambda i: (0, 0)),
                  row(D_MODEL), vec, vec],
        out_specs=row(D_MODEL),
        out_shape=jax.ShapeDtypeStruct((n, D_MODEL), F32),
        compiler_params=_params("parallel"),
        name="out_proj_ln",
    )(y_lru, y_na, y_mem, w, x, g, b)


FFN_TILE = 512
FFN_CHUNK = 512


def _ffn_kernel(x_ref, w1_ref, w3_ref, w2_ref, g_ref, b_ref, o_ref, xb_ref, acc_ref):
    j = pl.program_id(1)

    @pl.when(j == 0)
    def _():
        xb_ref[...] = x_ref[...].astype(BF16)
        acc_ref[...] = jnp.zeros_like(acc_ref)

    xb = xb_ref[...]
    h1 = jnp.dot(xb, w1_ref[...], preferred_element_type=F32)
    h3 = jnp.dot(xb, w3_ref[...], preferred_element_type=F32)
    h = (h1 * jax.nn.sigmoid(h1) * h3).astype(BF16)
    acc_ref[...] += jnp.dot(h, w2_ref[...], preferred_element_type=F32)

    @pl.when(j == pl.num_programs(1) - 1)
    def _():
        z = DN_ALPHA * x_ref[...] + acc_ref[...]
        o_ref[...] = _layer_norm_rows(z, g_ref[...], b_ref[...])


def _ffn(x, w1, w3, w2, g, b):
    n = x.shape[0]
    tm = min(FFN_TILE, n)
    fc = FFN_CHUNK
    vec = pl.BlockSpec((1, D_MODEL), lambda i, j: (0, 0))
    return pl.pallas_call(
        _ffn_kernel,
        grid=(n // tm, D_FF // fc),
        in_specs=[pl.BlockSpec((tm, D_MODEL), lambda i, j: (i, 0)),
                  pl.BlockSpec((D_MODEL, fc), lambda i, j: (0, j)),
                  pl.BlockSpec((D_MODEL, fc), lambda i, j: (0, j)),
                  pl.BlockSpec((fc, D_MODEL), lambda i, j: (j, 0)),
                  vec, vec],
        out_specs=pl.BlockSpec((tm, D_MODEL), lambda i, j: (i, 0)),
        out_shape=jax.ShapeDtypeStruct((n, D_MODEL), F32),
        scratch_shapes=[pltpu.VMEM((tm, D_MODEL), BF16), pltpu.VMEM((tm, D_MODEL), F32)],
        compiler_params=_params("parallel", "arbitrary"),
        name="ffn_ln",
    )(x, w1, w3, w2, g, b)


MOE_TILE = 512
EXPERT_TILE = 512
GATE_LANES = 128
TOP_K = 2
INFO_P1, INFO_P2, INFO_R1, INFO_R2, INFO_E1, INFO_E2 = range(6)


def _router_kernel(x_ref, rw_ref, info_ref, count_ref, run_ref):
    step = pl.program_id(0)

    @pl.when(step == 0)
    def _():
        run_ref[...] = jnp.zeros_like(run_ref)

    x = x_ref[...]
    x_hi = x.astype(BF16)
    x_lo = (x - x_hi.astype(F32)).astype(BF16)
    part = (jnp.dot(x_hi, rw_ref[...], preferred_element_type=F32)
            + jnp.dot(x_lo, rw_ref[...], preferred_element_type=F32))
    logits = part + pltpu.roll(part, GATE_LANES - N_EXPERTS, 1)
    tm = logits.shape[0]
    lane = lax.broadcasted_iota(jnp.int32, logits.shape, 1)
    logits = jnp.where(lane < N_EXPERTS, logits, -jnp.inf)
    m1 = jnp.max(logits, axis=-1, keepdims=True)
    i1 = jnp.min(jnp.where(logits == m1, lane, GATE_LANES), axis=-1, keepdims=True)
    rest = jnp.where(lane == i1, -jnp.inf, logits)
    m2 = jnp.max(rest, axis=-1, keepdims=True)
    i2 = jnp.min(jnp.where(rest == m2, lane, GATE_LANES), axis=-1, keepdims=True)
    e2 = jnp.exp(m2 - m1)
    p1 = 1.0 / (1.0 + e2)
    p2 = e2 / (1.0 + e2)

    sel = jnp.logical_or(lane == i1, lane == i2)
    onehot = jnp.where(sel, 1.0, 0.0).astype(BF16)
    r_i = lax.broadcasted_iota(jnp.int32, (tm, tm), 0)
    c_i = lax.broadcasted_iota(jnp.int32, (tm, tm), 1)
    lower = jnp.where(c_i < r_i, 1.0, 0.0).astype(BF16)
    rank = jnp.dot(lower, onehot, preferred_element_type=F32) + run_ref[0:1, :]
    run_ref[...] = run_ref[...] + jnp.sum(onehot.astype(F32), axis=0, keepdims=True)
    r1 = jnp.sum(jnp.where(lane == i1, rank, 0.0), axis=-1, keepdims=True)
    r2 = jnp.sum(jnp.where(lane == i2, rank, 0.0), axis=-1, keepdims=True)

    info = jnp.where(lane == INFO_P1, p1, 0.0)
    info = jnp.where(lane == INFO_P2, p2, info)
    info = jnp.where(lane == INFO_R1, r1, info)
    info = jnp.where(lane == INFO_R2, r2, info)
    info = jnp.where(lane == INFO_E1, i1.astype(F32), info)
    info = jnp.where(lane == INFO_E2, i2.astype(F32), info)
    info_ref[...] = info
    count_ref[...] = run_ref[...]


def _router(x, rw):
    n = x.shape[0]
    tm = min(MOE_TILE, n)
    return pl.pallas_call(
        _router_kernel,
        grid=(n // tm,),
        in_specs=[pl.BlockSpec((tm, D_MODEL), lambda i: (i, 0)),
                  pl.BlockSpec((D_MODEL, GATE_LANES), lambda i: (0, 0))],
        out_specs=[pl.BlockSpec((tm, GATE_LANES), lambda i: (i, 0)),
                   pl.BlockSpec((SUBLANES, GATE_LANES), lambda i: (0, 0))],
        out_shape=[jax.ShapeDtypeStruct((n, GATE_LANES), F32),
                   jax.ShapeDtypeStruct((SUBLANES, GATE_LANES), F32)],
        scratch_shapes=[pltpu.VMEM((SUBLANES, GATE_LANES), F32)],
        compiler_params=_params("arbitrary"),
        name="moe_router",
    )(x, rw)


def _row_copy(src, src_row, dst, dst_row, sem):
    return pltpu.make_async_copy(src.at[pl.ds(src_row, 1)], dst.at[pl.ds(dst_row, 1)], sem)


def _dispatch_kernel(fill_ref, x_ref, pos_hbm, xs_hbm, pos_smem, zero_ref, pos_sem, row_sem):
    step = pl.program_id(0)
    tm = x_ref.shape[0]

    @pl.when(step == 0)
    def _():
        zero_ref[...] = jnp.zeros_like(zero_ref)

        def clear_tile(start):
            start = pl.multiple_of(start, EXPERT_TILE)
            fill = pltpu.make_async_copy(zero_ref, xs_hbm.at[pl.ds(start, EXPERT_TILE)], row_sem)
            fill.start()
            fill.wait()

        for e in range(N_EXPERTS):
            @pl.when(fill_ref[e] >= 0)
            def _():
                clear_tile(fill_ref[e])

        def clear_idle(t, _):
            clear_tile(t * EXPERT_TILE)
            return 0

        lax.fori_loop(fill_ref[N_EXPERTS], xs_hbm.shape[0] // EXPERT_TILE, clear_idle, 0)

    base = pl.multiple_of(step * (TOP_K * tm), TOP_K * tm)
    pos_copy = pltpu.make_async_copy(pos_hbm.at[pl.ds(base, TOP_K * tm)], pos_smem, pos_sem)
    pos_copy.start()
    pos_copy.wait()

    def issue(r, _):
        for k in range(TOP_K):
            _row_copy(x_ref, r, xs_hbm, pos_smem[TOP_K * r + k], row_sem).start()
        return 0

    lax.fori_loop(0, tm, issue, 0, unroll=ROW_DMA_UNROLL)
    for k in range(TOP_K):
        pltpu.make_async_copy(x_ref, xs_hbm.at[pl.ds(0, tm)], row_sem).wait()


def _dispatch(x, pos, fill_start, n_rows):
    n = x.shape[0]
    tm = min(MOE_TILE, n)
    return pl.pallas_call(
        _dispatch_kernel,
        grid_spec=pltpu.PrefetchScalarGridSpec(
            num_scalar_prefetch=1,
            grid=(n // tm,),
            in_specs=[pl.BlockSpec((tm, D_MODEL), lambda i, f: (i, 0)),
                      pl.BlockSpec(memory_space=pl.ANY)],
            out_specs=pl.BlockSpec(memory_space=pl.ANY),
            scratch_shapes=[pltpu.SMEM((TOP_K * tm,), jnp.int32),
                            pltpu.VMEM((EXPERT_TILE, D_MODEL), F32),
                            pltpu.SemaphoreType.DMA(()), pltpu.SemaphoreType.DMA(())]),
        out_shape=jax.ShapeDtypeStruct((n_rows, D_MODEL), F32),
        compiler_params=_params("arbitrary"),
        name="moe_dispatch",
    )(fill_start, x, pos)


def _experts_kernel(te_ref, nv_ref, xs_ref, w1_ref, w3_ref, w2_ref, ys_ref):
    i = pl.program_id(0)

    @pl.when(i < nv_ref[0])
    def _():
        xb = xs_ref[...].astype(BF16)
        h1 = jnp.dot(xb, w1_ref[0], preferred_element_type=F32)
        h3 = jnp.dot(xb, w3_ref[0], preferred_element_type=F32)
        h = (h1 * jax.nn.sigmoid(h1) * h3).astype(BF16)
        ys_ref[...] = jnp.dot(h, w2_ref[0], preferred_element_type=F32)

    @pl.when(i >= nv_ref[0])
    def _():
        ys_ref[...] = jnp.zeros_like(ys_ref)


def _experts(xs, tile_expert, n_valid, w1, w3, w2):
    n_rows = xs.shape[0]
    te = EXPERT_TILE
    wspec = lambda shape: pl.BlockSpec((1,) + shape, lambda i, t, nv: (t[i], 0, 0))
    return pl.pallas_call(
        _experts_kernel,
        grid_spec=pltpu.PrefetchScalarGridSpec(
            num_scalar_prefetch=2,
            grid=(n_rows // te,),
            in_specs=[pl.BlockSpec((te, D_MODEL), lambda i, t, nv: (jnp.minimum(i, nv[0] - 1), 0)),
                      wspec((D_MODEL, D_FF_EXPERT)), wspec((D_MODEL, D_FF_EXPERT)),
                      wspec((D_FF_EXPERT, D_MODEL))],
            out_specs=pl.BlockSpec((te, D_MODEL), lambda i, t, nv: (i, 0))),
        out_shape=jax.ShapeDtypeStruct((n_rows, D_MODEL), F32),
        compiler_params=_params("arbitrary"),
        name="moe_experts",
    )(tile_expert, n_valid, xs, w1, w3, w2)


COMBINE_ROWS = 64


def _combine_kernel(x_ref, info_ref, g_ref, b_ref, pos_hbm, ys_hbm, o_ref,
                    pos_smem, ya_ref, yb_ref, pos_sem, row_sem):
    step = pl.program_id(0)
    n_tiles = pl.num_programs(0) - 1
    tm = x_ref.shape[0]

    @pl.when(step == 0)
    def _():
        yb_ref[...] = jnp.zeros_like(yb_ref)

    tile = jnp.minimum(step, n_tiles - 1)
    base = pl.multiple_of(tile * (TOP_K * tm), TOP_K * tm)
    pos_copy = pltpu.make_async_copy(pos_hbm.at[pl.ds(base, TOP_K * tm)], pos_smem, pos_sem)
    pos_copy.start()
    pos_copy.wait()
    g = g_ref[...]
    b = b_ref[...]

    def wait_rows(y_ref, sem):
        for k in range(TOP_K):
            pltpu.make_async_copy(ys_hbm.at[pl.ds(0, tm)], y_ref.at[k], sem).wait()

    def run(gather_ref, gather_sem, norm_ref, norm_sem):
        @pl.when(step > 0)
        def _():
            wait_rows(norm_ref, norm_sem)

        def block(s, _):
            r0 = pl.multiple_of(s * COMBINE_ROWS, COMBINE_ROWS)
            for r in range(COMBINE_ROWS):
                for k in range(TOP_K):
                    _row_copy(ys_hbm, pos_smem[TOP_K * (r0 + r) + k], gather_ref.at[k], r0 + r,
                              gather_sem).start()
            rows = pl.ds(r0, COMBINE_ROWS)
            info = info_ref[rows, :]
            p1 = info[:, INFO_P1:INFO_P1 + 1]
            p2 = info[:, INFO_P2:INFO_P2 + 1]
            z = DN_ALPHA * x_ref[rows, :] + (p1 * norm_ref[0, rows, :] + p2 * norm_ref[1, rows, :])
            o_ref[rows, :] = _layer_norm_rows(z, g, b)
            return 0

        lax.fori_loop(0, tm // COMBINE_ROWS, block, 0)

        @pl.when(step == n_tiles)
        def _():
            wait_rows(gather_ref, gather_sem)

    @pl.when(step % 2 == 0)
    def _():
        run(ya_ref, row_sem.at[0], yb_ref, row_sem.at[1])

    @pl.when(step % 2 == 1)
    def _():
        run(yb_ref, row_sem.at[1], ya_ref, row_sem.at[0])


def _combine(x, info, pos, ys, g, b):
    n = x.shape[0]
    tm = min(MOE_TILE, n)
    lagged = lambda i: (jnp.maximum(i - 1, 0), 0)
    vec = pl.BlockSpec((1, D_MODEL), lambda i: (0, 0))
    return pl.pallas_call(
        _combine_kernel,
        grid=(n // tm + 1,),
        in_specs=[pl.BlockSpec((tm, D_MODEL), lagged),
                  pl.BlockSpec((tm, GATE_LANES), lagged),
                  vec, vec,
                  pl.BlockSpec(memory_space=pl.ANY), pl.BlockSpec(memory_space=pl.ANY)],
        out_specs=pl.BlockSpec((tm, D_MODEL), lagged),
        out_shape=jax.ShapeDtypeStruct((n, D_MODEL), F32),
        scratch_shapes=[pltpu.SMEM((TOP_K * tm,), jnp.int32),
                        pltpu.VMEM((TOP_K, tm, D_MODEL), F32),
                        pltpu.VMEM((TOP_K, tm, D_MODEL), F32),
                        pltpu.SemaphoreType.DMA(()), pltpu.SemaphoreType.DMA((2,))],
        compiler_params=_params("arbitrary"),
        name="moe_combine_ln",
    )(x, info, g, b, pos, ys)


def _moe(x, rw, w1, w3, w2, g, b):
    n = x.shape[0]
    te = EXPERT_TILE
    info, counts = _router(x, rw)
    cnt = counts[0, :N_EXPERTS].astype(jnp.int32)
    padded = ((cnt + te - 1) // te) * te
    ends = jnp.cumsum(padded)
    offsets = ends - padded
    experts = jnp.arange(N_EXPERTS, dtype=jnp.int32)

    def position(e_lane, r_lane):
        e = info[:, e_lane].astype(jnp.int32)
        r = info[:, r_lane].astype(jnp.int32)
        return jnp.sum(jnp.where(e[:, None] == experts[None, :], offsets[None, :], 0), axis=1) + r

    pos = jnp.stack([position(INFO_E1, INFO_R1), position(INFO_E2, INFO_R2)], axis=1)
    pos = pos.reshape(TOP_K * n)
    max_tiles = TOP_K * n // te + N_EXPERTS
    n_valid = (ends[-1] // te).astype(jnp.int32)
    tile_start = jnp.arange(max_tiles, dtype=jnp.int32) * te
    tile_start = jnp.minimum(tile_start, (n_valid - 1) * te)
    tile_expert = jnp.sum((tile_start[:, None] >= ends[None, :]).astype(jnp.int32), axis=1)
    fill_start = jnp.where(padded > 0, ends - te, -1).astype(jnp.int32)
    fill_start = jnp.concatenate([fill_start, n_valid.reshape(1)])

    xs = _dispatch(x, pos, fill_start, max_tiles * te)
    ys = _experts(xs, tile_expert, n_valid.reshape(1), w1, w3, w2)
    return _combine(x, info, pos, ys, g, b)


def _block_diag_gates(gate_w, gate_b):
    per = GATE_GROUP // LRU_BLOCK
    w = gate_w.reshape(2, N_GATE_GROUPS, per, LRU_BLOCK, LRU_BLOCK)
    eye = jnp.eye(per, dtype=gate_w.dtype)
    wbd = w[:, :, :, :, None, :] * eye[None, None, :, None, :, None]
    wbd = jnp.transpose(wbd, (1, 2, 3, 0, 4, 5)).reshape(N_GATE_GROUPS, GATE_GROUP, 2 * GATE_GROUP)
    b = gate_b.reshape(2, N_GATE_GROUPS, GATE_GROUP)
    b = jnp.transpose(b, (1, 0, 2)).reshape(N_GATE_GROUPS, 1, 2 * GATE_GROUP)
    return (0.5 * wbd).astype(BF16), (0.5 * b).astype(F32)


def _prepare_layer(l, w_in, conv_w, conv_b, lru_gate_w, lru_gate_b, lru_lambda, rpb, w_mem_kv,
                   mix_gain, w_out, ln1_g, ln1_b, ffn_w1, ffn_w3, ffn_w2, router_w, moe_w1,
                   moe_w3, moe_w2, ln2_g, ln2_b):
    p = {}
    p['w_in'] = w_in[l].astype(BF16)
    p['conv_w'] = conv_w[l].astype(F32)
    p['conv_b'] = conv_b[l].astype(F32).reshape(1, LRU_WIDTH)
    p['gates'] = [_block_diag_gates(lru_gate_w[l, d], lru_gate_b[l, d]) for d in range(2)]
    p['lam'] = [lru_lambda[l, d].astype(F32).reshape(1, LRU_WIDTH) for d in range(2)]
    p['na_bias'] = _natten_bias_table(rpb[l])
    p['w_mem_kv'] = w_mem_kv[l].astype(BF16)
    gain = mix_gain[l].astype(F32)
    p['gain_lru'] = gain[:LRU_WIDTH].reshape(1, LRU_WIDTH)
    p['gain_na'] = gain[LRU_WIDTH:LRU_WIDTH + NA_WIDTH].reshape(1, NA_WIDTH)
    p['gain_mem'] = gain[LRU_WIDTH + NA_WIDTH:].reshape(1, MEM_WIDTH)
    p['w_out'] = w_out[l].astype(BF16)
    vec = lambda v: v.astype(F32).reshape(1, D_MODEL)
    p['ln1'] = (vec(ln1_g[l]), vec(ln1_b[l]))
    p['ln2'] = (vec(ln2_g[l]), vec(ln2_b[l]))
    j = l // 2
    if l % 2 == 0:
        p['ffn'] = (ffn_w1[j].astype(BF16), ffn_w3[j].astype(BF16), ffn_w2[j].astype(BF16))
    else:
        w = router_w[j].astype(F32)
        w_hi = w.astype(BF16)
        w_lo = (w - w_hi.astype(F32)).astype(BF16)
        rw = jnp.zeros((D_MODEL, GATE_LANES), BF16)
        rw = rw.at[:, :N_EXPERTS].set(w_hi).at[:, N_EXPERTS:2 * N_EXPERTS].set(w_lo)
        p['moe'] = (rw, moe_w1[j].astype(BF16), moe_w3[j].astype(BF16), moe_w2[j].astype(BF16))
    return p


def _trunk(x, mem, layers):
    b, t, _ = x.shape
    n = b * t
    rows = t // GRID_W
    x = x.reshape(n, D_MODEL)
    mem2 = mem.reshape(b * N_MEM, D_MODEL)
    for p in layers:
        u_lru, u_att = _in_proj(x, p['w_in'])
        (wbd_f, gb_f), (wbd_b, gb_b) = p['gates']
        h_fwd, xc = _lru_direction(u_lru, p['conv_w'], p['conv_b'], wbd_f, gb_f, p['lam'][0],
                                   t, False)
        y_lru = _lru_direction(u_lru, p['conv_w'], p['conv_b'], wbd_b, gb_b, p['lam'][1], t, True,
                               xc=xc, h_fwd=h_fwd, gain=p['gain_lru'])
        y_na = _natten(u_att.reshape(b, rows, GRID_W, 2048), p['na_bias'], p['gain_na'])
        kv = _matmul(mem2, p['w_mem_kv'], BF16).reshape(b, N_MEM, 2 * MEM_WIDTH)
        y_mem = _mem_attn(u_att.reshape(b, t, 2048), kv, p['gain_mem'])
        x = _out_proj(y_lru, y_na.reshape(n, NA_WIDTH), y_mem.reshape(n, MEM_WIDTH),
                      p['w_out'], x, *p['ln1'])
        if 'ffn' in p:
            x = _ffn(x, *p['ffn'], *p['ln2'])
        else:
            x = _moe(x, *p['moe'], *p['ln2'])
    return x.reshape(b, t, D_MODEL)


def kernel(x_prompt, x_sample, mem_prompt, mem_sample, w_in, conv_w, conv_b, lru_gate_w,
           lru_gate_b, lru_lambda, rpb, w_mem_kv, mix_gain, w_out, ln1_g, ln1_b, ffn_w1,
           ffn_w3, ffn_w2, router_w, moe_w1, moe_w3, moe_w2, ln2_g, ln2_b):
    weights = (w_in, conv_w, conv_b, lru_gate_w, lru_gate_b, lru_lambda, rpb, w_mem_kv,
               mix_gain, w_out, ln1_g, ln1_b, ffn_w1, ffn_w3, ffn_w2, router_w, moe_w1,
               moe_w3, moe_w2, ln2_g, ln2_b)
    layers = [_prepare_layer(l, *weights) for l in range(DEPTH)]
    return (_trunk(x_prompt, mem_prompt, layers), _trunk(x_sample, mem_sample, layers))
```
